```python
import jax, jax.numpy as jnp
from jax import lax
import numpy as np

D_MODEL = 1024
BATCH = 16
SEQ = 2048
DEPTH = 1

PLE_DIM = 256
SSM_EXPAND = 2
D_INNER = SSM_EXPAND * D_MODEL
SSM_HEAD_DIM = 64
SSM_HEADS = D_INNER // SSM_HEAD_DIM
SSM_GROUPS = 4
HEADS_PER_GROUP = SSM_HEADS // SSM_GROUPS
D_STATE = 128
CONV_WIDTH = 4
CHUNK = 128
D_CONV_CH = D_INNER + 2 * SSM_GROUPS * D_STATE
D_POOL = D_MODEL
POOL_WINDOWS = (2, 4, 8, 16)
POOL_GROUPS = len(POOL_WINDOWS)
POOL_GROUP_DIM = D_POOL // POOL_GROUPS
IN_SPLITS = (D_INNER, D_CONV_CH, SSM_HEADS, D_POOL, D_POOL, D_MODEL, D_MODEL)
N_IN = sum(IN_SPLITS)
EPS = 1e-6

kernel_name = "hybrid_ssd_multipool_gated_merge"


def rms_norm(x, g):
    xf = x.astype(jnp.float32)
    y = xf * lax.rsqrt(jnp.mean(xf * xf, axis=-1, keepdims=True) + EPS)
    return (y * g.astype(jnp.float32)).astype(x.dtype)


def causal_depthwise_conv(u, w, b):
    out = lax.conv_general_dilated(
        u, w[:, None, :].astype(u.dtype), window_strides=(1,),
        padding=[(CONV_WIDTH - 1, 0)],
        dimension_numbers=("NWC", "WIO", "NWC"),
        feature_group_count=u.shape[-1])
    return out + b


def ssd_chunked(xs, dt, a_neg, bm, cm):
    bsz, seqlen = xs.shape[0], xs.shape[1]
    nc = seqlen // CHUNK
    dtype = xs.dtype
    x = xs.reshape(bsz, nc, CHUNK, SSM_GROUPS, HEADS_PER_GROUP, SSM_HEAD_DIM)
    dtc = dt.reshape(bsz, nc, CHUNK, SSM_GROUPS, HEADS_PER_GROUP)
    bc = bm.reshape(bsz, nc, CHUNK, SSM_GROUPS, D_STATE)
    cc = cm.reshape(bsz, nc, CHUNK, SSM_GROUPS, D_STATE)
    xdt = x * dtc[..., None]
    a_dt = dtc.astype(jnp.float32) * a_neg.astype(jnp.float32).reshape(SSM_GROUPS, HEADS_PER_GROUP)
    a_cs = jnp.cumsum(a_dt, axis=2)

    seg = a_cs[:, :, :, None] - a_cs[:, :, None, :]
    mask = jnp.tril(jnp.ones((CHUNK, CHUNK), dtype=bool))[:, :, None, None]
    decay_ls = jnp.exp(jnp.where(mask, seg, -jnp.inf)).astype(dtype)
    cb = jnp.einsum("bclgn,bcsgn->bclsg", cc, bc)
    m = cb[..., None] * decay_ls
    y_diag = jnp.einsum("bclsge,bcsgep->bclgep", m, xdt)

    decay_to_end = jnp.exp(a_cs[:, :, -1:] - a_cs).astype(dtype)
    states = jnp.einsum("bclgn,bclgep->bcgepn", bc, xdt * decay_to_end[..., None])
    chunk_decay = jnp.exp(a_cs[:, :, -1]).astype(dtype)

    def step(h, inp):
        s_c, d_c = inp
        return h * d_c[..., None, None] + s_c, h

    h0 = jnp.zeros_like(states[:, 0])
    _, prev = lax.scan(step, h0, (jnp.moveaxis(states, 1, 0), jnp.moveaxis(chunk_decay, 1, 0)))
    prev = jnp.moveaxis(prev, 0, 1)

    y_off = jnp.einsum("bclgn,bcgepn->bclgep", cc, prev) * jnp.exp(a_cs).astype(dtype)[..., None]
    return (y_diag + y_off).reshape(bsz, seqlen, D_INNER)


def causal_multiscale_pool(u, mix_w, mix_b, scale):
    bsz, seqlen = u.shape[0], u.shape[1]
    ug = u.reshape(bsz, seqlen, POOL_GROUPS, POOL_GROUP_DIM)
    csum = jnp.cumsum(ug.astype(jnp.float32), axis=1)
    pos = jnp.arange(1, seqlen + 1, dtype=jnp.float32)
    means = []
    for gi, w in enumerate(POOL_WINDOWS):
        cs = csum[:, :, gi]
        lag = jnp.pad(cs, ((0, 0), (w, 0), (0, 0)))[:, :seqlen]
        means.append((cs - lag) / jnp.minimum(pos, float(w))[None, :, None])
    pooled = jnp.stack(means, axis=2).astype(u.dtype) - ug
    mixed = jnp.einsum("bsgc,gcd->bsgd", pooled, mix_w).reshape(bsz, seqlen, D_POOL) + mix_b
    return mixed * scale


def setup_inputs(seed: int = 0) -> dict:
    key = jax.random.key(seed)
    ks = jax.random.split(key, 20)
    f32 = jnp.float32
    nrm = lambda k, shape, s: jax.random.normal(k, shape, f32) * s
    dt0 = jnp.exp(jax.random.uniform(ks[6], (DEPTH, SSM_HEADS), f32, np.log(1e-3), np.log(1e-1)))
    return {
        "x": nrm(ks[0], (BATCH, SEQ, D_MODEL), 1.0),
        "p": nrm(ks[1], (DEPTH, BATCH, SEQ, PLE_DIM), 1.0),
        "norm_g": 1.0 + nrm(ks[2], (DEPTH, D_MODEL), 0.05),
        "w_in": nrm(ks[3], (DEPTH, D_MODEL, N_IN), D_MODEL ** -0.5),
        "conv_w": nrm(ks[4], (DEPTH, CONV_WIDTH, D_CONV_CH), CONV_WIDTH ** -0.5),
        "conv_b": nrm(ks[5], (DEPTH, D_CONV_CH), 0.02),
        "dt_bias": dt0 + jnp.log(-jnp.expm1(-dt0)),
        "a_log": jnp.log(jax.random.uniform(ks[7], (DEPTH, SSM_HEADS), f32, 1.0, 16.0)),
        "d_skip": 1.0 + nrm(ks[8], (DEPTH, SSM_HEADS), 0.1),
        "gnorm_g": 1.0 + nrm(ks[9], (DEPTH, D_INNER), 0.05),
        "pool_mix_w": nrm(ks[10], (DEPTH, POOL_GROUPS, POOL_GROUP_DIM, POOL_GROUP_DIM), POOL_GROUP_DIM ** -0.5),
        "pool_mix_b": nrm(ks[11], (DEPTH, D_POOL), 0.02),
        "pool_scale": 1.0 + nrm(ks[12], (DEPTH, D_POOL), 0.1),
        "w_branch_a": nrm(ks[13], (DEPTH, D_INNER, D_MODEL), D_INNER ** -0.5),
        "w_branch_b": nrm(ks[14], (DEPTH, D_POOL, D_MODEL), D_POOL ** -0.5),
        "w_out": nrm(ks[15], (DEPTH, D_MODEL, D_MODEL), D_MODEL ** -0.5),
        "ple_norm_g": 1.0 + nrm(ks[16], (DEPTH, D_MODEL), 0.05),
        "w_ple_gate": nrm(ks[17], (DEPTH, D_MODEL, D_MODEL), D_MODEL ** -0.5),
        "w_ple_up": nrm(ks[18], (DEPTH, PLE_DIM, D_MODEL), PLE_DIM ** -0.5),
        "final_g": 1.0 + nrm(ks[19], (D_MODEL,), 0.05),
    }


def reference(x, p, norm_g, w_in, conv_w, conv_b, dt_bias, a_log, d_skip, gnorm_g,
              pool_mix_w, pool_mix_b, pool_scale, w_branch_a, w_branch_b, w_out,
              ple_norm_g, w_ple_gate, w_ple_up, final_g):
    split_idx = [int(v) for v in np.cumsum(IN_SPLITS)[:-1]]
    bsz, seqlen = x.shape[0], x.shape[1]
    for i in range(DEPTH):
        h = rms_norm(x, norm_g[i])
        proj = h @ w_in[i]
        z, xbc, dt_raw, u, z_pool, g_a, g_b = jnp.split(proj, split_idx, axis=-1)

        xbc = jax.nn.silu(causal_depthwise_conv(xbc, conv_w[i], conv_b[i]))
        x_ssm, b_ssm, c_ssm = jnp.split(xbc, [D_INNER, D_INNER + SSM_GROUPS * D_STATE], axis=-1)
        dt = jax.nn.softplus(dt_raw + dt_bias[i])
        a_neg = -jnp.exp(a_log[i])
        y_a = ssd_chunked(x_ssm, dt, a_neg, b_ssm, c_ssm)
        y_a = y_a + x_ssm * jnp.repeat(d_skip[i], SSM_HEAD_DIM)
        y_a = (y_a * jax.nn.silu(z)).reshape(bsz, seqlen, SSM_GROUPS, D_INNER // SSM_GROUPS)
        y_a = rms_norm(y_a, gnorm_g[i].reshape(SSM_GROUPS, D_INNER // SSM_GROUPS)).reshape(bsz, seqlen, D_INNER)

        y_b = causal_multiscale_pool(u, pool_mix_w[i], pool_mix_b[i], pool_scale[i]) * jax.nn.silu(z_pool)

        merged = jax.nn.sigmoid(g_a) * (y_a @ w_branch_a[i]) + jax.nn.sigmoid(g_b) * (y_b @ w_branch_b[i])
        x = x + merged @ w_out[i]

        gate = jax.nn.sigmoid(rms_norm(x, ple_norm_g[i]) @ w_ple_gate[i])
        x = x + gate * (p[i] @ w_ple_up[i])
    return rms_norm(x, final_g)
```

```python
import functools

import jax
import jax.numpy as jnp
from jax import lax
from jax.experimental import pallas as pl
from jax.experimental.pallas import tpu as pltpu

D_MODEL = 1024
PLE_DIM = 256
D_INNER = 2048
HEAD_DIM = 64
N_HEADS = 32
N_GROUPS = 4
HEADS_PER_GROUP = 8
D_STATE = 128
CONV_WIDTH = 4
D_CONV_CH = D_INNER + 2 * N_GROUPS * D_STATE
D_POOL = 1024
POOL_WINDOWS = (2, 4, 8, 16)
POOL_GROUP_DIM = 256
GROUP_WIDTH = D_INNER // N_GROUPS
EPS = 1e-6

LANES = 128
SUBLANES = 8
VMEM_LIMIT_BYTES = 56 * 1024 * 1024

CHUNK = 128
CONV_TAIL = SUBLANES
POOL_TAIL = 16

OFF_XBC, OFF_U, OFF_Z, OFF_ZP, OFF_GA, OFF_GB = 0, 3072, 4096, 6144, 7168, 8192
N_MAIN = 9216
IN_TN = 1024


def _sigmoid(v):
    return 1.0 / (1.0 + jnp.exp(-v))


def _rms_scale(v):
    return v * lax.rsqrt(jnp.mean(v * v, axis=-1, keepdims=True) + EPS)


def _in_proj_kernel(x_ref, g_ref, w_ref, wdt_ref, out_ref, dt_ref, h_ref):
    @pl.when(pl.program_id(1) == 0)
    def _():
        h = (_rms_scale(x_ref[...]) * g_ref[...]).astype(jnp.bfloat16)
        h_ref[...] = h
        dt_ref[...] = jnp.dot(h, wdt_ref[...], preferred_element_type=jnp.float32)

    out_ref[...] = jnp.dot(h_ref[...], w_ref[...],
                           preferred_element_type=jnp.float32).astype(out_ref.dtype)


def _in_proj(x2d, norm_g, w_main, w_dt, tm):
    t = x2d.shape[0]
    return pl.pallas_call(
        _in_proj_kernel,
        grid=(t // tm, N_MAIN // IN_TN),
        in_specs=[
            pl.BlockSpec((tm, D_MODEL), lambda i, j: (i, 0)),
            pl.BlockSpec((1, D_MODEL), lambda i, j: (0, 0)),
            pl.BlockSpec((D_MODEL, IN_TN), lambda i, j: (0, j)),
            pl.BlockSpec((D_MODEL, LANES), lambda i, j: (0, 0)),
        ],
        out_specs=[
            pl.BlockSpec((tm, IN_TN), lambda i, j: (i, j)),
            pl.BlockSpec((tm, LANES), lambda i, j: (i, 0)),
        ],
        out_shape=[
            jax.ShapeDtypeStruct((t, N_MAIN), jnp.bfloat16),
            jax.ShapeDtypeStruct((t, LANES), jnp.float32),
        ],
        scratch_shapes=[pltpu.VMEM((tm, D_MODEL), jnp.bfloat16)],
        compiler_params=pltpu.CompilerParams(
            dimension_semantics=("arbitrary", "arbitrary"),
            vmem_limit_bytes=VMEM_LIMIT_BYTES),
        name="in_proj",
    )(x2d, norm_g, w_main, w_dt)


def _shift_rows(v, k):
    rows = lax.broadcasted_iota(jnp.int32, v.shape, 0)
    return jnp.where(rows >= k, pltpu.roll(v, k, axis=0), 0.0)


def _mixer_kernel(
        x_ref, p_ref, xbc_ref, u_ref, z_ref, zp_ref, ga_ref, gb_ref, dt_ref,
        convw_ref, convb_ref, dtb_ref, alog_ref, dskip_ref, gng_ref, expand_ref,
        mixw_ref, mixb_ref, pscale_ref, wa_ref, wb_ref, wout_ref,
        pleg_ref, wgate_ref, wup_ref, fing_ref,
        out_ref,
        xpad_ref, upad_ref, state_ref, act_ref, dtx_ref, ya_ref):
    c = pl.program_id(1)
    f32, bf16 = jnp.float32, jnp.bfloat16
    L = CHUNK

    @pl.when(c == 0)
    def _():
        xpad_ref[0:CONV_TAIL, :] = jnp.zeros((CONV_TAIL, D_CONV_CH), f32)
        upad_ref[0:POOL_TAIL, :] = jnp.zeros((POOL_TAIL, D_POOL), f32)
        state_ref[...] = jnp.zeros(state_ref.shape, f32)

    xpad_ref[CONV_TAIL:CONV_TAIL + L, :] = xbc_ref[...].astype(f32)
    conv = convb_ref[...]
    for k in range(CONV_WIDTH):
        off = CONV_TAIL - (CONV_WIDTH - 1) + k
        conv = conv + xpad_ref[off:off + L, :] * convw_ref[k:k + 1, :]
    xpad_ref[0:CONV_TAIL, :] = xpad_ref[L:L + CONV_TAIL, :]
    act_ref[...] = conv * _sigmoid(conv)

    dtv = dt_ref[...] + dtb_ref[...]
    dtv = jnp.maximum(dtv, 0.0) + jnp.log1p(jnp.exp(-jnp.abs(dtv)))
    a_neg = -jnp.exp(alog_ref[...])
    head_lane = lax.broadcasted_iota(jnp.int32, (1, LANES), 1) < N_HEADS
    acs = jnp.where(head_lane, dtv * a_neg, 0.0)
    k = 1
    while k < L:
        acs = acs + _shift_rows(acs, k)
        k *= 2
    acs_t = acs.T

    dt_hi = dtv.astype(bf16)
    r1 = dtv - dt_hi.astype(f32)
    dt_mid = r1.astype(bf16)
    dt_lo = (r1 - dt_mid.astype(f32)).astype(bf16)
    dt_split = jnp.concatenate([dt_hi, dt_mid, dt_lo], axis=1)
    dtx_ref[...] = jnp.dot(dt_split, expand_ref[...], preferred_element_type=f32)

    rows = lax.broadcasted_iota(jnp.int32, (L, L), 0)
    cols = lax.broadcasted_iota(jnp.int32, (L, L), 1)
    causal = rows >= cols
    first_half = lax.broadcasted_iota(jnp.int32, (L, LANES), 1) < HEAD_DIM

    for g in range(N_GROUPS):
        b_g = act_ref[:, D_INNER + g * D_STATE:D_INNER + (g + 1) * D_STATE]
        c_g = act_ref[:, D_INNER + N_GROUPS * D_STATE + g * D_STATE:
                      D_INNER + N_GROUPS * D_STATE + (g + 1) * D_STATE]
        b_bf, c_bf = b_g.astype(bf16), c_g.astype(bf16)
        cb = lax.dot_general(c_bf, b_bf, (((1,), (1,)), ((), ())), preferred_element_type=f32)
        cb = jnp.where(causal, cb, 0.0)
        h_prev = state_ref[g]
        y_off = jnp.dot(c_bf, h_prev.astype(bf16), preferred_element_type=f32)

        a_last_g, xw_g, y_g = [], [], []
        for j in range(HEADS_PER_GROUP // 2):
            ha = g * HEADS_PER_GROUP + 2 * j
            col0 = ha * HEAD_DIM
            lo = j * LANES
            col_a = jnp.broadcast_to(acs[:, ha:ha + 1], (L, L))
            col_b = jnp.broadcast_to(acs[:, ha + 1:ha + 2], (L, L))
            row_a = acs_t[ha:ha + 1, :]
            row_b = acs_t[ha + 1:ha + 2, :]
            m_a = (cb * jnp.exp(jnp.minimum(col_a - row_a, 0.0))).astype(bf16)
            m_b = (cb * jnp.exp(jnp.minimum(col_b - row_b, 0.0))).astype(bf16)

            a_exp = jnp.where(first_half, col_a, col_b)
            a_last = a_exp[L - 1:L, :]
            xs = act_ref[:, col0:col0 + LANES]
            xdt = xs * dtx_ref[:, col0:col0 + LANES]
            xdt_bf = xdt.astype(bf16)
            xw_g.append((xdt * jnp.exp(a_last - a_exp)).astype(bf16))

            zero = jnp.zeros_like(xdt_bf)
            rhs = jnp.concatenate([jnp.where(first_half, xdt_bf, zero),
                                   jnp.where(first_half, zero, xdt_bf)], axis=0)
            lhs = jnp.concatenate([m_a, m_b], axis=1)
            y = jnp.dot(lhs, rhs, preferred_element_type=f32)
            y = y + y_off[:, lo:lo + LANES] * jnp.exp(a_exp)
            y = y + xs * dskip_ref[:, col0:col0 + LANES]
            zz = z_ref[:, col0:col0 + LANES].astype(f32)
            y_g.append(y * (zz * _sigmoid(zz)))
            a_last_g.append(a_last)

        chunk_decay = jnp.exp(jnp.concatenate(a_last_g, axis=1))
        upd = jnp.dot(b_g.T.astype(bf16), jnp.concatenate(xw_g, axis=1),
                      preferred_element_type=f32)
        state_ref[g] = h_prev * chunk_decay + upd

        yg = _rms_scale(jnp.concatenate(y_g, axis=1)) * gng_ref[:, g * GROUP_WIDTH:(g + 1) * GROUP_WIDTH]
        ya_ref[:, g * GROUP_WIDTH:(g + 1) * GROUP_WIDTH] = yg.astype(bf16)

    upad_ref[POOL_TAIL:POOL_TAIL + L, :] = u_ref[...].astype(f32)
    pos = lax.broadcasted_iota(jnp.int32, (L, POOL_GROUP_DIM), 0) + (c * L + 1)
    mixed = []
    for gi, w in enumerate(POOL_WINDOWS):
        cs = slice(gi * POOL_GROUP_DIM, (gi + 1) * POOL_GROUP_DIM)
        ug = upad_ref[POOL_TAIL:POOL_TAIL + L, cs]
        acc = ug
        for s in range(1, w):
            acc = acc + upad_ref[POOL_TAIL - s:POOL_TAIL - s + L, cs]
        cnt = jnp.minimum(pos, w).astype(f32)
        pooled = (acc / cnt - ug).astype(bf16)
        mixed.append(jnp.dot(pooled, mixw_ref[gi], preferred_element_type=f32))
    upad_ref[0:POOL_TAIL, :] = upad_ref[L:L + POOL_TAIL, :]
    yb = (jnp.concatenate(mixed, axis=1) + mixb_ref[...]) * pscale_ref[...]
    zp = zp_ref[...].astype(f32)
    yb = (yb * (zp * _sigmoid(zp))).astype(bf16)

    proj_a = jnp.dot(ya_ref[...], wa_ref[...], preferred_element_type=f32)
    proj_b = jnp.dot(yb, wb_ref[...], preferred_element_type=f32)
    merged = (_sigmoid(ga_ref[...].astype(f32)) * proj_a
              + _sigmoid(gb_ref[...].astype(f32)) * proj_b).astype(bf16)
    x1 = x_ref[0] + jnp.dot(merged, wout_ref[...], preferred_element_type=f32)

    hg = (_rms_scale(x1) * pleg_ref[...]).astype(bf16)
    gate = _sigmoid(jnp.dot(hg, wgate_ref[...], preferred_element_type=f32))
    up = jnp.dot(p_ref[0].astype(bf16), wup_ref[...], preferred_element_type=f32)
    x2 = x1 + gate * up
    out_ref[0] = _rms_scale(x2) * fing_ref[...]


def _full(shape):
    nd = len(shape)
    return pl.BlockSpec(shape, lambda b, c: (0,) * nd)


def _mixer(x, p0, proj, dt_raw, params):
    bsz, seqlen, _ = x.shape
    nc = seqlen // CHUNK

    def seg(width, off):
        blk = off // width
        return pl.BlockSpec((CHUNK, width), lambda b, c: (b * nc + c, blk))

    in_specs = [
        pl.BlockSpec((1, CHUNK, D_MODEL), lambda b, c: (b, c, 0)),
        pl.BlockSpec((1, CHUNK, PLE_DIM), lambda b, c: (b, c, 0)),
        seg(D_CONV_CH, OFF_XBC), seg(D_POOL, OFF_U), seg(D_INNER, OFF_Z),
        seg(D_POOL, OFF_ZP), seg(D_MODEL, OFF_GA), seg(D_MODEL, OFF_GB),
        pl.BlockSpec((CHUNK, LANES), lambda b, c: (b * nc + c, 0)),
    ] + [_full(a.shape) for a in params]
    return pl.pallas_call(
        _mixer_kernel,
        grid=(bsz, nc),
        in_specs=in_specs,
        out_specs=pl.BlockSpec((1, CHUNK, D_MODEL), lambda b, c: (b, c, 0)),
        out_shape=jax.ShapeDtypeStruct(x.shape, jnp.float32),
        scratch_shapes=[
            pltpu.VMEM((CONV_TAIL + CHUNK, D_CONV_CH), jnp.float32),
            pltpu.VMEM((POOL_TAIL + CHUNK, D_POOL), jnp.float32),
            pltpu.VMEM((N_GROUPS, D_STATE, GROUP_WIDTH), jnp.float32),
            pltpu.VMEM((CHUNK, D_CONV_CH), jnp.float32),
            pltpu.VMEM((CHUNK, D_INNER), jnp.float32),
            pltpu.VMEM((CHUNK, D_INNER), jnp.bfloat16),
        ],
        compiler_params=pltpu.CompilerParams(
            dimension_semantics=("arbitrary", "arbitrary"),
            vmem_limit_bytes=VMEM_LIMIT_BYTES),
        name="mixer",
    )(x, p0, proj, proj, proj, proj, proj, proj, dt_raw, *params)


def _row(v, width=None):
    v = v.reshape(1, -1).astype(jnp.float32)
    if width is not None and v.shape[1] < width:
        v = jnp.pad(v, ((0, 0), (0, width - v.shape[1])))
    return v


def kernel(x, p, norm_g, w_in, conv_w, conv_b, dt_bias, a_log, d_skip, gnorm_g, pool_mix_w,
           pool_mix_b, pool_scale, w_branch_a, w_branch_b, w_out, ple_norm_g, w_ple_gate,
           w_ple_up, final_g):
    depth = w_in.shape[0]
    bsz, seqlen, _ = x.shape
    assert seqlen % CHUNK == 0
    bf16 = jnp.bfloat16
    o_z, o_xbc, o_dt = 0, D_INNER, D_INNER + D_CONV_CH
    o_u = o_dt + N_HEADS
    o_zp, o_ga, o_gb = o_u + D_POOL, o_u + 2 * D_POOL, o_u + 3 * D_POOL
    head_of_col = jnp.arange(D_INNER) // HEAD_DIM
    expand1 = (jnp.arange(LANES)[:, None] == head_of_col[None, :]).astype(bf16)
    expand = jnp.concatenate([expand1, expand1, expand1], axis=0)

    t = bsz * seqlen
    tm = 2048 if t % 2048 == 0 else CHUNK
    for i in range(depth):
        w = w_in[i]
        w_main = jnp.concatenate([
            w[:, o_xbc:o_xbc + D_CONV_CH], w[:, o_u:o_u + D_POOL], w[:, o_z:o_z + D_INNER],
            w[:, o_zp:o_zp + D_POOL], w[:, o_ga:o_ga + D_MODEL], w[:, o_gb:o_gb + D_MODEL],
        ], axis=1).astype(bf16)
        w_dt = jnp.pad(w[:, o_dt:o_dt + N_HEADS], ((0, 0), (0, LANES - N_HEADS))).astype(bf16)
        proj, dt_raw = _in_proj(x.reshape(t, D_MODEL), _row(norm_g[i]), w_main, w_dt, tm)
        params = (
            conv_w[i].astype(jnp.float32), _row(conv_b[i]), _row(dt_bias[i], LANES),
            _row(a_log[i], LANES), _row(jnp.repeat(d_skip[i], HEAD_DIM)), _row(gnorm_g[i]),
            expand, pool_mix_w[i].astype(bf16), _row(pool_mix_b[i]), _row(pool_scale[i]),
            w_branch_a[i].astype(bf16), w_branch_b[i].astype(bf16), w_out[i].astype(bf16),
            _row(ple_norm_g[i]), w_ple_gate[i].astype(bf16), w_ple_up[i].astype(bf16),
            _row(final_g),
        )
        assert depth == 1
        x = _mixer(x, p[i], proj, dt_raw, params)
    return x
```

```python
import math

import jax
import jax.numpy as jnp
from jax import lax
from jax.experimental import pallas as pl
from jax.experimental.pallas import tpu as pltpu

D_MODEL = 1024
PLE_DIM = 256
D_INNER = 2048
HEAD_DIM = 64
N_HEADS = 32
N_GROUPS = 4
HEADS_PER_GROUP = 8
D_STATE = 128
CONV_WIDTH = 4
D_CONV_CH = D_INNER + 2 * N_GROUPS * D_STATE
D_POOL = 1024
POOL_WINDOWS = (2, 4, 8, 16)
POOL_GROUP_DIM = 256
GROUP_WIDTH = D_INNER // N_GROUPS
EPS = 1e-6
LOG2E = math.log2(math.e)

LANES = 128
SUBLANES = 8
VMEM_LIMIT_BYTES = 56 * 1024 * 1024

CHUNK = 128
STEP = 2 * CHUNK
CONV_TAIL = SUBLANES
POOL_TAIL = 3 * SUBLANES
CONV_SLABS = D_CONV_CH // LANES
POOL_SLABS_PER_GROUP = POOL_GROUP_DIM // LANES

OFF_XBC, OFF_U, OFF_Z, OFF_ZP, OFF_GA, OFF_GB = 0, 3072, 4096, 6144, 7168, 8192
N_MAIN = 9216
IN_TN = 1024


def _sigmoid(v):
    return 0.5 + 0.5 * jnp.tanh(0.5 * v)


def _silu(v):
    h = 0.5 * v
    return h + h * jnp.tanh(h)


def _rms_scale(v):
    return v * lax.rsqrt(jnp.mean(v * v, axis=-1, keepdims=True) + EPS)


def _in_proj_kernel(x_ref, g_ref, w_ref, wdt_ref, out_ref, dt_ref, h_ref):
    @pl.when(pl.program_id(1) == 0)
    def _():
        h = (_rms_scale(x_ref[...]) * g_ref[...]).astype(jnp.bfloat16)
        h_ref[...] = h
        dt_ref[...] = jnp.dot(h, wdt_ref[...], preferred_element_type=jnp.float32)

    out_ref[...] = jnp.dot(h_ref[...], w_ref[...],
                           preferred_element_type=jnp.float32).astype(out_ref.dtype)


def _in_proj(x2d, norm_g, w_main, w_dt, tm):
    t = x2d.shape[0]
    return pl.pallas_call(
        _in_proj_kernel,
        grid=(t // tm, N_MAIN // IN_TN),
        in_specs=[
            pl.BlockSpec((tm, D_MODEL), lambda i, j: (i, 0)),
            pl.BlockSpec((1, D_MODEL), lambda i, j: (0, 0)),
            pl.BlockSpec((D_MODEL, IN_TN), lambda i, j: (0, j)),
            pl.BlockSpec((D_MODEL, LANES), lambda i, j: (0, 0)),
        ],
        out_specs=[
            pl.BlockSpec((tm, IN_TN), lambda i, j: (i, j)),
            pl.BlockSpec((tm, LANES), lambda i, j: (i, 0)),
        ],
        out_shape=[
            jax.ShapeDtypeStruct((t, N_MAIN), jnp.bfloat16),
            jax.ShapeDtypeStruct((t, LANES), jnp.float32),
        ],
        scratch_shapes=[pltpu.VMEM((tm, D_MODEL), jnp.bfloat16)],
        compiler_params=pltpu.CompilerParams(
            dimension_semantics=("arbitrary", "arbitrary"),
            vmem_limit_bytes=VMEM_LIMIT_BYTES),
        name="in_proj",
    )(x2d, norm_g, w_main, w_dt)


def _shift_rows(v, k):
    rows = lax.broadcasted_iota(jnp.int32, v.shape, 0)
    return jnp.where(rows >= k, pltpu.roll(v, k, axis=0), 0.0)


def _conv_silu(xbc_ref, convw_ref, convb_ref, xpad_ref, act_ref):
    t = STEP
    for s in range(CONV_SLABS):
        cs = slice(s * LANES, (s + 1) * LANES)
        xpad_ref[s, CONV_TAIL:CONV_TAIL + t, :] = xbc_ref[:, cs].astype(jnp.float32)
        conv = convb_ref[:, cs]
        for k in range(CONV_WIDTH):
            off = CONV_TAIL - (CONV_WIDTH - 1) + k
            conv = conv + xpad_ref[s, off:off + t, :] * convw_ref[k:k + 1, cs]
        xpad_ref[s, 0:CONV_TAIL, :] = xpad_ref[s, t:t + CONV_TAIL, :]
        act_ref[:, cs] = _silu(conv)


def _pool_sums(u_ref, upad_ref, tmp4_ref, tmp8_ref):
    t, h = STEP, POOL_TAIL
    sums, centers = [], []
    for s in range(D_POOL // LANES):
        cs = slice(s * LANES, (s + 1) * LANES)
        upad_ref[s, h:h + t, :] = u_ref[:, cs].astype(jnp.float32)

    def win(ref, s, start, n, w):
        acc = ref[s, start:start + n, :]
        for k in range(1, w):
            acc = acc + ref[s, start - k:start - k + n, :]
        return acc

    for s in range(D_POOL // LANES):
        w = POOL_WINDOWS[s // POOL_SLABS_PER_GROUP]
        centers.append(upad_ref[s, h:h + t, :])
        if w <= 4:
            sums.append(win(upad_ref, s, h, t, w))
        elif w == 8:
            i4 = s - 2 * POOL_SLABS_PER_GROUP
            tmp4_ref[i4, 8:16 + t, :] = win(upad_ref, s, h - 8, t + 8, 4)
            sums.append(tmp4_ref[i4, 16:16 + t, :] + tmp4_ref[i4, 12:12 + t, :])
        else:
            i4 = s - 2 * POOL_SLABS_PER_GROUP
            i8 = s - 3 * POOL_SLABS_PER_GROUP
            tmp4_ref[i4, 0:16 + t, :] = win(upad_ref, s, h - 16, t + 16, 4)
            tmp8_ref[i8, 0:8 + t, :] = tmp4_ref[i4, 8:16 + t, :] + tmp4_ref[i4, 4:12 + t, :]
            sums.append(tmp8_ref[i8, 8:8 + t, :] + tmp8_ref[i8, 0:t, :])
    for s in range(D_POOL // LANES):
        upad_ref[s, 0:h, :] = upad_ref[s, t:t + h, :]
    return sums, centers


def _ssd_chunk(r0, dtv, dtx_ref, act_ref, z_ref, dskip_ref, gng_ref, alog_ref, state_ref, ya_ref):
    f32, bf16 = jnp.float32, jnp.bfloat16
    L = CHUNK
    rs = slice(r0, r0 + L)
    a_neg = -jnp.exp(alog_ref[...]) * LOG2E
    head_lane = lax.broadcasted_iota(jnp.int32, (1, LANES), 1) < N_HEADS
    acs = jnp.where(head_lane, dtv * a_neg, 0.0)
    k = 1
    while k < L:
        acs = acs + _shift_rows(acs, k)
        k *= 2
    acs_t = acs.T

    rows = lax.broadcasted_iota(jnp.int32, (L, L), 0)
    cols = lax.broadcasted_iota(jnp.int32, (L, L), 1)
    causal = rows >= cols
    first_half = lax.broadcasted_iota(jnp.int32, (L, LANES), 1) < HEAD_DIM

    for g in range(N_GROUPS):
        b_g = act_ref[rs, D_INNER + g * D_STATE:D_INNER + (g + 1) * D_STATE]
        c_g = act_ref[rs, D_INNER + N_GROUPS * D_STATE + g * D_STATE:
                      D_INNER + N_GROUPS * D_STATE + (g + 1) * D_STATE]
        b_bf, c_bf = b_g.astype(bf16), c_g.astype(bf16)
        cb = lax.dot_general(c_bf, b_bf, (((1,), (1,)), ((), ())), preferred_element_type=f32)
        cb = jnp.where(causal, cb, 0.0)
        h_prev = state_ref[g]
        y_off = jnp.dot(c_bf, h_prev.astype(bf16), preferred_element_type=f32)

        a_last_g, xw_g, y_g = [], [], []
        for j in range(HEADS_PER_GROUP // 2):
            ha = g * HEADS_PER_GROUP + 2 * j
            col0 = ha * HEAD_DIM
            lo = j * LANES
            col_a = jnp.broadcast_to(acs[:, ha:ha + 1], (L, L))
            col_b = jnp.broadcast_to(acs[:, ha + 1:ha + 2], (L, L))
            row_a = acs_t[ha:ha + 1, :]
            row_b = acs_t[ha + 1:ha + 2, :]
            m_a = (cb * jnp.exp2(jnp.minimum(col_a - row_a, 0.0))).astype(bf16)
            m_b = (cb * jnp.exp2(jnp.minimum(col_b - row_b, 0.0))).astype(bf16)

            a_exp = jnp.where(first_half, col_a, col_b)
            a_last = a_exp[L - 1:L, :]
            xs = act_ref[rs, col0:col0 + LANES]
            xdt = xs * dtx_ref[rs, col0:col0 + LANES]
            xdt_bf = xdt.astype(bf16)
            xw_g.append((xdt * jnp.exp2(a_last - a_exp)).astype(bf16))

            zero = jnp.zeros_like(xdt_bf)
            rhs = jnp.concatenate([jnp.where(first_half, xdt_bf, zero),
                                   jnp.where(first_half, zero, xdt_bf)], axis=0)
            lhs = jnp.concatenate([m_a, m_b], axis=1)
            y = jnp.dot(lhs, rhs, preferred_element_type=f32)
            y = y + y_off[:, lo:lo + LANES] * jnp.exp2(a_exp)
            y = y + xs * dskip_ref[:, col0:col0 + LANES]
            zz = z_ref[rs, col0:col0 + LANES].astype(f32)
            y_g.append(y * _silu(zz))
            a_last_g.append(a_last)

        chunk_decay = jnp.exp2(jnp.concatenate(a_last_g, axis=1))
        upd = jnp.dot(b_g.T.astype(bf16), jnp.concatenate(xw_g, axis=1),
                      preferred_element_type=f32)
        state_ref[g] = h_prev * chunk_decay + upd

        yg = _rms_scale(jnp.concatenate(y_g, axis=1)) * gng_ref[:, g * GROUP_WIDTH:(g + 1) * GROUP_WIDTH]
        ya_ref[rs, g * GROUP_WIDTH:(g + 1) * GROUP_WIDTH] = yg.astype(bf16)


def _mixer_kernel(
        x_ref, p_ref, xbc_ref, u_ref, z_ref, zp_ref, ga_ref, gb_ref, dt_ref,
        convw_ref, convb_ref, dtb_ref, alog_ref, dskip_ref, gng_ref, expand_ref,
        mixw_ref, mixb_ref, pscale_ref, wa_ref, wb_ref, wout_ref,
        pleg_ref, wgate_ref, wup_ref, fing_ref,
        out_ref,
        xpad_ref, upad_ref, tmp4_ref, tmp8_ref, state_ref, act_ref, dtx_ref, ya_ref):
    c = pl.program_id(1)
    f32, bf16 = jnp.float32, jnp.bfloat16

    @pl.when(c == 0)
    def _():
        xpad_ref[:, 0:CONV_TAIL, :] = jnp.zeros((CONV_SLABS, CONV_TAIL, LANES), f32)
        upad_ref[:, 0:POOL_TAIL, :] = jnp.zeros((D_POOL // LANES, POOL_TAIL, LANES), f32)
        state_ref[...] = jnp.zeros(state_ref.shape, f32)

    _conv_silu(xbc_ref, convw_ref, convb_ref, xpad_ref, act_ref)

    dtv = dt_ref[...] + dtb_ref[...]
    dtv = jnp.maximum(dtv, 0.0) + jnp.log1p(jnp.exp(-jnp.abs(dtv)))
    dt_hi = dtv.astype(bf16)
    r1 = dtv - dt_hi.astype(f32)
    dt_mid = r1.astype(bf16)
    dt_lo = (r1 - dt_mid.astype(f32)).astype(bf16)
    dt_split = jnp.concatenate([dt_hi, dt_mid, dt_lo], axis=1)
    dtx_ref[...] = jnp.dot(dt_split, expand_ref[...], preferred_element_type=f32)

    for q in range(STEP // CHUNK):
        r0 = q * CHUNK
        _ssd_chunk(r0, dtv[r0:r0 + CHUNK], dtx_ref, act_ref, z_ref, dskip_ref, gng_ref,
                   alog_ref, state_ref, ya_ref)

    sums, centers = _pool_sums(u_ref, upad_ref, tmp4_ref, tmp8_ref)
    pos = lax.broadcasted_iota(jnp.int32, (STEP, LANES), 0) + (c * STEP + 1)
    mixed = []
    for gi, w in enumerate(POOL_WINDOWS):
        cnt = jnp.minimum(pos, w).astype(f32)
        pooled = [(sums[s] / cnt - centers[s]).astype(bf16)
                  for s in range(gi * POOL_SLABS_PER_GROUP, (gi + 1) * POOL_SLABS_PER_GROUP)]
        mixed.append(jnp.dot(jnp.concatenate(pooled, axis=1), mixw_ref[gi],
                             preferred_element_type=f32))
    yb = (jnp.concatenate(mixed, axis=1) + mixb_ref[...]) * pscale_ref[...]
    zp = zp_ref[...].astype(f32)
    yb = (yb * _silu(zp)).astype(bf16)

    proj_a = jnp.dot(ya_ref[...], wa_ref[...], preferred_element_type=f32)
    proj_b = jnp.dot(yb, wb_ref[...], preferred_element_type=f32)
    merged = (_sigmoid(ga_ref[...].astype(f32)) * proj_a
              + _sigmoid(gb_ref[...].astype(f32)) * proj_b).astype(bf16)
    x1 = x_ref[0] + jnp.dot(merged, wout_ref[...], preferred_element_type=f32)

    hg = (_rms_scale(x1) * pleg_ref[...]).astype(bf16)
    gate = _sigmoid(jnp.dot(hg, wgate_ref[...], preferred_element_type=f32))
    up = jnp.dot(p_ref[0].astype(bf16), wup_ref[...], preferred_element_type=f32)
    x2 = x1 + gate * up
    out_ref[0] = _rms_scale(x2) * fing_ref[...]


def _full(shape):
    nd = len(shape)
    return pl.BlockSpec(shape, lambda b, c: (0,) * nd)


def _mixer(x, p0, proj, dt_raw, params):
    bsz, seqlen, _ = x.shape
    ns = seqlen // STEP

    def seg(width, off):
        blk = off // width
        return pl.BlockSpec((STEP, width), lambda b, c: (b * ns + c, blk))

    in_specs = [
        pl.BlockSpec((1, STEP, D_MODEL), lambda b, c: (b, c, 0)),
        pl.BlockSpec((1, STEP, PLE_DIM), lambda b, c: (b, c, 0)),
        seg(D_CONV_CH, OFF_XBC), seg(D_POOL, OFF_U), seg(D_INNER, OFF_Z),
        seg(D_POOL, OFF_ZP), seg(D_MODEL, OFF_GA), seg(D_MODEL, OFF_GB),
        pl.BlockSpec((STEP, LANES), lambda b, c: (b * ns + c, 0)),
    ] + [_full(a.shape) for a in params]
    return pl.pallas_call(
        _mixer_kernel,
        grid=(bsz, ns),
        in_specs=in_specs,
        out_specs=pl.BlockSpec((1, STEP, D_MODEL), lambda b, c: (b, c, 0)),
        out_shape=jax.ShapeDtypeStruct(x.shape, jnp.float32),
        scratch_shapes=[
            pltpu.VMEM((CONV_SLABS, CONV_TAIL + STEP, LANES), jnp.float32),
            pltpu.VMEM((D_POOL // LANES, POOL_TAIL + STEP, LANES), jnp.float32),
            pltpu.VMEM((2 * POOL_SLABS_PER_GROUP, 16 + STEP, LANES), jnp.float32),
            pltpu.VMEM((POOL_SLABS_PER_GROUP, 8 + STEP, LANES), jnp.float32),
            pltpu.VMEM((N_GROUPS, D_STATE, GROUP_WIDTH), jnp.float32),
            pltpu.VMEM((STEP, D_CONV_CH), jnp.float32),
            pltpu.VMEM((STEP, D_INNER), jnp.float32),
            pltpu.VMEM((STEP, D_INNER), jnp.bfloat16),
        ],
        compiler_params=pltpu.CompilerParams(
            dimension_semantics=("arbitrary", "arbitrary"),
            vmem_limit_bytes=VMEM_LIMIT_BYTES),
        name="mixer",
    )(x, p0, proj, proj, proj, proj, proj, proj, dt_raw, *params)


def _row(v, width=None):
    v = v.reshape(1, -1).astype(jnp.float32)
    if width is not None and v.shape[1] < width:
        v = jnp.pad(v, ((0, 0), (0, width - v.shape[1])))
    return v


def kernel(x, p, norm_g, w_in, conv_w, conv_b, dt_bias, a_log, d_skip, gnorm_g, pool_mix_w,
           pool_mix_b, pool_scale, w_branch_a, w_branch_b, w_out, ple_norm_g, w_ple_gate,
           w_ple_up, final_g):
    depth = w_in.shape[0]
    bsz, seqlen, _ = x.shape
    assert seqlen % STEP == 0
    bf16 = jnp.bfloat16
    o_z, o_xbc, o_dt = 0, D_INNER, D_INNER + D_CONV_CH
    o_u = o_dt + N_HEADS
    o_zp, o_ga, o_gb = o_u + D_POOL, o_u + 2 * D_POOL, o_u + 3 * D_POOL
    head_of_col = jnp.arange(D_INNER) // HEAD_DIM
    expand1 = (jnp.arange(LANES)[:, None] == head_of_col[None, :]).astype(bf16)
    expand = jnp.concatenate([expand1, expand1, expand1], axis=0)

    t = bsz * seqlen
    tm = 2048 if t % 2048 == 0 else STEP
    for i in range(depth):
        w = w_in[i]
        w_main = jnp.concatenate([
            w[:, o_xbc:o_xbc + D_CONV_CH], w[:, o_u:o_u + D_POOL], w[:, o_z:o_z + D_INNER],
            w[:, o_zp:o_zp + D_POOL], w[:, o_ga:o_ga + D_MODEL], w[:, o_gb:o_gb + D_MODEL],
        ], axis=1).astype(bf16)
        w_dt = jnp.pad(w[:, o_dt:o_dt + N_HEADS], ((0, 0), (0, LANES - N_HEADS))).astype(bf16)
        proj, dt_raw = _in_proj(x.reshape(t, D_MODEL), _row(norm_g[i]), w_main, w_dt, tm)
        params = (
            conv_w[i].astype(jnp.float32), _row(conv_b[i]), _row(dt_bias[i], LANES),
            _row(a_log[i], LANES), _row(jnp.repeat(d_skip[i], HEAD_DIM)), _row(gnorm_g[i]),
            expand, pool_mix_w[i].astype(bf16), _row(pool_mix_b[i]), _row(pool_scale[i]),
            w_branch_a[i].astype(bf16), w_branch_b[i].astype(bf16), w_out[i].astype(bf16),
            _row(ple_norm_g[i]), w_ple_gate[i].astype(bf16), w_ple_up[i].astype(bf16),
            _row(final_g),
        )
        assert depth == 1
        x = _mixer(x, p[i], proj, dt_raw, params)
    return x
```

```python
import functools
import math

import jax
import jax.numpy as jnp
from jax import lax
from jax.experimental import pallas as pl
from jax.experimental.pallas import tpu as pltpu

D_MODEL = 1024
PLE_DIM = 256
D_INNER = 2048
HEAD_DIM = 64
N_HEADS = 32
N_GROUPS = 4
HEADS_PER_GROUP = 8
D_STATE = 128
CONV_WIDTH = 4
D_CONV_CH = D_INNER + 2 * N_GROUPS * D_STATE
D_POOL = 1024
POOL_WINDOWS = (2, 4, 8, 16)
POOL_GROUP_DIM = 256
GROUP_WIDTH = D_INNER // N_GROUPS
EPS = 1e-6
LOG2E = math.log2(math.e)

LANES = 128
SUBLANES = 8
VMEM_BYTES_V7X = 64 * 1024 * 1024
VMEM_LIMIT_BYTES = VMEM_BYTES_V7X - 3 * 1024 * 1024

CHUNK = 128
STEP = 2 * CHUNK
CONV_TAIL = SUBLANES
POOL_TAIL = 3 * SUBLANES
CONV_SLABS = D_CONV_CH // LANES
POOL_SLABS = D_POOL // LANES
POOL_SLABS_PER_GROUP = POOL_GROUP_DIM // LANES

OFF_U = D_CONV_CH
OFF_GATES = OFF_U + D_POOL
N_GATES = D_INNER + D_POOL
N_MAIN = OFF_GATES + N_GATES
G_Z, G_ZP = 0, D_INNER
TAIL_ROWS = 512
PROJ_TN = 512
TAIL_TN = 256


def _sigmoid(v):
    return 0.5 + 0.5 * jnp.tanh(0.5 * v)


def _silu(v):
    h = 0.5 * v
    return h + h * jnp.tanh(h)


def _rms_scale(v):
    return v * lax.rsqrt(jnp.mean(v * v, axis=-1, keepdims=True) + EPS)


def _shift_rows(v, k):
    rows = lax.broadcasted_iota(jnp.int32, v.shape, 0)
    return jnp.where(rows >= k, pltpu.roll(v, k, axis=0), 0.0)


def _project(x, normg_ref, wmain_ref, wdt_ref, bufs):
    xpad_ref, upad_ref, gates_ref, dt_ref = bufs
    f32 = jnp.float32
    h = (_rms_scale(x) * normg_ref[...]).astype(jnp.bfloat16)
    slabs_per_tile = PROJ_TN // LANES
    for j in range(N_MAIN // PROJ_TN):
        col = j * PROJ_TN
        tile = jnp.dot(h, wmain_ref[:, col:col + PROJ_TN], preferred_element_type=f32)
        if col < OFF_GATES:
            for k in range(slabs_per_tile):
                s = j * slabs_per_tile + k
                piece = tile[:, k * LANES:(k + 1) * LANES]
                if s < CONV_SLABS:
                    xpad_ref[s, CONV_TAIL:CONV_TAIL + STEP, :] = piece
                else:
                    upad_ref[s - CONV_SLABS, POOL_TAIL:POOL_TAIL + STEP, :] = piece
        else:
            gates_ref[:, col - OFF_GATES:col - OFF_GATES + PROJ_TN] = tile.astype(gates_ref.dtype)
    dt_ref[...] = jnp.dot(h, wdt_ref[...], preferred_element_type=f32)


def _conv_silu(convw_ref, convb_ref, xpad_ref, s, r0):
    cs = slice(s * LANES, (s + 1) * LANES)
    conv = convb_ref[:, cs]
    for k in range(CONV_WIDTH):
        off = r0 + CONV_TAIL - (CONV_WIDTH - 1) + k
        conv = conv + xpad_ref[s, off:off + CHUNK, :] * convw_ref[k:k + 1, cs]
    return _silu(conv)


def _pool_sums(upad_ref, upad_next_ref, tmp4_ref, tmp8_ref):
    t, h = STEP, POOL_TAIL

    def win(ref, s, start, n, w):
        acc = ref[s, start:start + n, :]
        for k in range(1, w):
            acc = acc + ref[s, start - k:start - k + n, :]
        return acc

    sums, centers = [], []
    for s in range(POOL_SLABS):
        w = POOL_WINDOWS[s // POOL_SLABS_PER_GROUP]
        centers.append(upad_ref[s, h:h + t, :])
        if w <= 4:
            sums.append(win(upad_ref, s, h, t, w))
        elif w == 8:
            i4 = s - 2 * POOL_SLABS_PER_GROUP
            tmp4_ref[i4, 8:16 + t, :] = win(upad_ref, s, h - 8, t + 8, 4)
            sums.append(tmp4_ref[i4, 16:16 + t, :] + tmp4_ref[i4, 12:12 + t, :])
        else:
            i4 = s - 2 * POOL_SLABS_PER_GROUP
            i8 = s - 3 * POOL_SLABS_PER_GROUP
            tmp4_ref[i4, 0:16 + t, :] = win(upad_ref, s, h - 16, t + 16, 4)
            tmp8_ref[i8, 0:8 + t, :] = tmp4_ref[i4, 8:16 + t, :] + tmp4_ref[i4, 4:12 + t, :]
            sums.append(tmp8_ref[i8, 8:8 + t, :] + tmp8_ref[i8, 0:t, :])
        upad_next_ref[s, 0:h, :] = upad_ref[s, t:t + h, :]
    return sums, centers


def _ssd_chunk(r0, dtv, xpad_ref, gates_ref, w, state_ref, ya_ref):
    f32, bf16 = jnp.float32, jnp.bfloat16
    L = CHUNK
    rs = slice(r0, r0 + L)
    conv = functools.partial(_conv_silu, w["conv_w"], w["conv_b"], xpad_ref)
    a_neg = -jnp.exp(w["a_log"][...]) * LOG2E
    head_lane = lax.broadcasted_iota(jnp.int32, (1, LANES), 1) < N_HEADS
    acs = jnp.where(head_lane, dtv * a_neg, 0.0)
    k = 1
    while k < L:
        acs = acs + _shift_rows(acs, k)
        k *= 2
    acs_t = acs.T

    rows = lax.broadcasted_iota(jnp.int32, (L, L), 0)
    cols = lax.broadcasted_iota(jnp.int32, (L, L), 1)
    causal = rows >= cols
    first_half = lax.broadcasted_iota(jnp.int32, (L, LANES), 1) < HEAD_DIM

    for g in range(N_GROUPS):
        b_g = conv(D_INNER // LANES + g, r0)
        c_g = conv(D_INNER // LANES + N_GROUPS + g, r0)
        b_bf, c_bf = b_g.astype(bf16), c_g.astype(bf16)
        cb = lax.dot_general(c_bf, b_bf, (((1,), (1,)), ((), ())), preferred_element_type=f32)
        cb = jnp.where(causal, cb, 0.0)
        h_prev = state_ref[g]
        y_off = jnp.dot(c_bf, h_prev.astype(bf16), preferred_element_type=f32)

        a_last_g, xw_g, y_g = [], [], []
        for j in range(HEADS_PER_GROUP // 2):
            ha = g * HEADS_PER_GROUP + 2 * j
            col0 = ha * HEAD_DIM
            lo = j * LANES
            col_a = jnp.broadcast_to(acs[:, ha:ha + 1], (L, L))
            col_b = jnp.broadcast_to(acs[:, ha + 1:ha + 2], (L, L))
            row_a = acs_t[ha:ha + 1, :]
            row_b = acs_t[ha + 1:ha + 2, :]
            m_a = (cb * jnp.exp2(jnp.minimum(col_a - row_a, 0.0))).astype(bf16)
            m_b = (cb * jnp.exp2(jnp.minimum(col_b - row_b, 0.0))).astype(bf16)

            a_exp = jnp.where(first_half, col_a, col_b)
            a_last = a_exp[L - 1:L, :]
            dt_exp = jnp.where(first_half, jnp.broadcast_to(dtv[:, ha:ha + 1], (L, LANES)),
                               jnp.broadcast_to(dtv[:, ha + 1:ha + 2], (L, LANES)))
            xs = conv(col0 // LANES, r0)
            xdt = xs * dt_exp
            xdt_bf = xdt.astype(bf16)
            xw_g.append((xdt * jnp.exp2(a_last - a_exp)).astype(bf16))

            zero = jnp.zeros_like(xdt_bf)
            rhs = jnp.concatenate([jnp.where(first_half, xdt_bf, zero),
                                   jnp.where(first_half, zero, xdt_bf)], axis=0)
            lhs = jnp.concatenate([m_a, m_b], axis=1)
            y = jnp.dot(lhs, rhs, preferred_element_type=f32)
            y = y + y_off[:, lo:lo + LANES] * jnp.exp2(a_exp)
            y = y + xs * w["d_skip"][:, col0:col0 + LANES]
            zz = gates_ref[rs, G_Z + col0:G_Z + col0 + LANES].astype(f32)
            y_g.append(y * _silu(zz))
            a_last_g.append(a_last)

        chunk_decay = jnp.exp2(jnp.concatenate(a_last_g, axis=1))
        upd = jnp.dot(b_g.T.astype(bf16), jnp.concatenate(xw_g, axis=1),
                      preferred_element_type=f32)
        state_ref[g] = h_prev * chunk_decay + upd

        gn = w["gnorm_g"][:, g * GROUP_WIDTH:(g + 1) * GROUP_WIDTH]
        yg = _rms_scale(jnp.concatenate(y_g, axis=1)) * gn
        ya_ref[rs, g * GROUP_WIDTH:(g + 1) * GROUP_WIDTH] = yg.astype(bf16)


def _mix(c, cur, nxt, w, ya_ref, yb_ref, scr):
    f32, bf16 = jnp.float32, jnp.bfloat16
    xpad_ref, upad_ref, gates_ref, dt_ref = cur
    tmp4_ref, tmp8_ref, state_ref = scr

    dtv = dt_ref[...] + w["dt_bias"][...]
    dtv = jnp.maximum(dtv, 0.0) + jnp.log1p(jnp.exp(-jnp.abs(dtv)))
    for q in range(STEP // CHUNK):
        r0 = q * CHUNK
        _ssd_chunk(r0, dtv[r0:r0 + CHUNK], xpad_ref, gates_ref, w, state_ref, ya_ref)
    for s in range(CONV_SLABS):
        nxt[0][s, 0:CONV_TAIL, :] = xpad_ref[s, STEP:STEP + CONV_TAIL, :]

    sums, centers = _pool_sums(upad_ref, nxt[1], tmp4_ref, tmp8_ref)
    pos = lax.broadcasted_iota(jnp.int32, (STEP, LANES), 0) + (c * STEP + 1)
    for gi, win in enumerate(POOL_WINDOWS):
        gs = slice(gi * POOL_GROUP_DIM, (gi + 1) * POOL_GROUP_DIM)
        cnt = jnp.minimum(pos, win).astype(f32)
        pooled = [(sums[s] / cnt - centers[s]).astype(bf16)
                  for s in range(gi * POOL_SLABS_PER_GROUP, (gi + 1) * POOL_SLABS_PER_GROUP)]
        mixed = jnp.dot(jnp.concatenate(pooled, axis=1), w["mix_w"][gi], preferred_element_type=f32)
        yb = (mixed + w["mix_b"][:, gs]) * w["pool_scale"][:, gs]
        zp = gates_ref[:, G_ZP + gi * POOL_GROUP_DIM:G_ZP + (gi + 1) * POOL_GROUP_DIM].astype(f32)
        yb_ref[:, gs] = (yb * _silu(zp)).astype(bf16)


MIXER_WEIGHTS = ("norm_g", "w_main", "w_dt", "conv_w", "conv_b", "dt_bias", "a_log", "d_skip",
                 "gnorm_g", "mix_w", "mix_b", "pool_scale")


def _mixer_kernel(steps_per_seq, x_ref, xn_ref, *refs):
    nw = len(MIXER_WEIGHTS)
    w = dict(zip(MIXER_WEIGHTS, refs[:nw]))
    ya_ref, yb_ref = refs[nw:nw + 2]
    xpad0, upad0, gates0, dt0, xpad1, upad1, gates1, dt1 = refs[nw + 2:nw + 10]
    scr = refs[nw + 10:]
    state_ref = scr[2]
    sets = ((xpad0, upad0, gates0, dt0), (xpad1, upad1, gates1, dt1))
    i = pl.program_id(0)
    c = lax.rem(i, steps_per_seq)

    @pl.when(i == 0)
    def _():
        _project(x_ref[...], w["norm_g"], w["w_main"], w["w_dt"], sets[0])

    @pl.when(c == 0)
    def _():
        xpad0[:, 0:CONV_TAIL, :] = jnp.zeros((CONV_SLABS, CONV_TAIL, LANES), jnp.float32)
        upad0[:, 0:POOL_TAIL, :] = jnp.zeros((POOL_SLABS, POOL_TAIL, LANES), jnp.float32)
        state_ref[...] = jnp.zeros(state_ref.shape, jnp.float32)

    for parity in range(2):
        @pl.when(lax.rem(i, 2) == parity)
        def _():
            cur, nxt = sets[parity], sets[1 - parity]
            _project(xn_ref[...], w["norm_g"], w["w_main"], w["w_dt"], nxt)
            _mix(c, cur, nxt, w, ya_ref, yb_ref, scr)


def _const_spec(a):
    return pl.BlockSpec(a.shape, lambda i: (0,) * a.ndim, pipeline_mode=pl.Buffered(1))


def _mixer_call(x2d, weights, steps_per_seq):
    t = x2d.shape[0]
    n = t // STEP
    assert steps_per_seq % 2 == 0
    buf_set = [
        pltpu.VMEM((CONV_SLABS, CONV_TAIL + STEP, LANES), jnp.float32),
        pltpu.VMEM((POOL_SLABS, POOL_TAIL + STEP, LANES), jnp.float32),
        pltpu.VMEM((STEP, N_GATES), jnp.bfloat16),
        pltpu.VMEM((STEP, LANES), jnp.float32),
    ]
    return pl.pallas_call(
        functools.partial(_mixer_kernel, steps_per_seq),
        grid=(n,),
        in_specs=[
            pl.BlockSpec((STEP, D_MODEL), lambda i: (0, 0)),
            pl.BlockSpec((STEP, D_MODEL), lambda i: (jnp.minimum(i + 1, n - 1), 0)),
        ] + [_const_spec(a) for a in weights],
        out_specs=[
            pl.BlockSpec((STEP, D_INNER), lambda i: (i, 0)),
            pl.BlockSpec((STEP, D_POOL), lambda i: (i, 0)),
        ],
        out_shape=[
            jax.ShapeDtypeStruct((t, D_INNER), jnp.bfloat16),
            jax.ShapeDtypeStruct((t, D_POOL), jnp.bfloat16),
        ],
        scratch_shapes=buf_set + buf_set + [
            pltpu.VMEM((2 * POOL_SLABS_PER_GROUP, 16 + STEP, LANES), jnp.float32),
            pltpu.VMEM((POOL_SLABS_PER_GROUP, 8 + STEP, LANES), jnp.float32),
            pltpu.VMEM((N_GROUPS, D_STATE, GROUP_WIDTH), jnp.float32),
        ],
        compiler_params=pltpu.CompilerParams(
            dimension_semantics=("arbitrary",),
            vmem_limit_bytes=VMEM_LIMIT_BYTES),
        name="mixer",
    )(x2d, x2d, *weights)


TAIL_WEIGHTS = ("norm_g", "w_ga", "w_gb", "w_a", "w_b", "w_out", "ple_g", "w_gate", "w_up",
                "final_g")


def _tail_kernel(x_ref, p_ref, ya_ref, yb_ref, *refs):
    nw = len(TAIL_WEIGHTS)
    w = dict(zip(TAIL_WEIGHTS, refs[:nw]))
    out_ref, h_ref, merged_ref, hg_ref = refs[nw:]
    f32, bf16 = jnp.float32, jnp.bfloat16
    rows = x_ref.shape[0]
    tiles = [slice(n * TAIL_TN, (n + 1) * TAIL_TN) for n in range(D_MODEL // TAIL_TN)]

    h_ref[...] = (_rms_scale(x_ref[...]) * w["norm_g"][...]).astype(bf16)
    for ts in tiles:
        g_a = jnp.dot(h_ref[...], w["w_ga"][:, ts], preferred_element_type=f32)
        g_b = jnp.dot(h_ref[...], w["w_gb"][:, ts], preferred_element_type=f32)
        proj_a = jnp.dot(ya_ref[...], w["w_a"][:, ts], preferred_element_type=f32)
        proj_b = jnp.dot(yb_ref[...], w["w_b"][:, ts], preferred_element_type=f32)
        merged_ref[:, ts] = (_sigmoid(g_a) * proj_a + _sigmoid(g_b) * proj_b).astype(bf16)
    ssq = jnp.zeros((rows, 1), f32)
    for ts in tiles:
        x1 = x_ref[:, ts] + jnp.dot(merged_ref[...], w["w_out"][:, ts], preferred_element_type=f32)
        out_ref[:, ts] = x1
        ssq = ssq + jnp.sum(x1 * x1, axis=-1, keepdims=True)

    inv = lax.rsqrt(ssq * (1.0 / D_MODEL) + EPS)
    for ts in tiles:
        hg_ref[:, ts] = (out_ref[:, ts] * inv * w["ple_g"][:, ts]).astype(bf16)
    p_bf = p_ref[...].astype(bf16)
    ssq = jnp.zeros((rows, 1), f32)
    for ts in tiles:
        gate = _sigmoid(jnp.dot(hg_ref[...], w["w_gate"][:, ts], preferred_element_type=f32))
        up = jnp.dot(p_bf, w["w_up"][:, ts], preferred_element_type=f32)
        x2 = out_ref[:, ts] + gate * up
        out_ref[:, ts] = x2
        ssq = ssq + jnp.sum(x2 * x2, axis=-1, keepdims=True)
    inv = lax.rsqrt(ssq * (1.0 / D_MODEL) + EPS)
    for ts in tiles:
        out_ref[:, ts] = out_ref[:, ts] * inv * w["final_g"][:, ts]


def _tail_call(x2d, p2d, ya, yb, weights):
    t = x2d.shape[0]
    tb = TAIL_ROWS if t % TAIL_ROWS == 0 else STEP
    row_spec = lambda width: pl.BlockSpec((tb, width), lambda i: (i, 0))
    return pl.pallas_call(
        _tail_kernel,
        grid=(t // tb,),
        in_specs=[row_spec(D_MODEL), row_spec(PLE_DIM), row_spec(D_INNER), row_spec(D_POOL)]
        + [_const_spec(a) for a in weights],
        out_specs=row_spec(D_MODEL),
        out_shape=jax.ShapeDtypeStruct((t, D_MODEL), jnp.float32),
        scratch_shapes=[
            pltpu.VMEM((tb, D_MODEL), jnp.bfloat16),
            pltpu.VMEM((tb, D_MODEL), jnp.bfloat16),
            pltpu.VMEM((tb, D_MODEL), jnp.bfloat16),
        ],
        compiler_params=pltpu.CompilerParams(
            dimension_semantics=("arbitrary",),
            vmem_limit_bytes=VMEM_LIMIT_BYTES),
        name="tail",
    )(x2d, p2d, ya, yb, *weights)


def _row(v, width=None):
    v = v.reshape(1, -1).astype(jnp.float32)
    if width is not None and v.shape[1] < width:
        v = jnp.pad(v, ((0, 0), (0, width - v.shape[1])))
    return v


def kernel(x, p, norm_g, w_in, conv_w, conv_b, dt_bias, a_log, d_skip, gnorm_g, pool_mix_w,
           pool_mix_b, pool_scale, w_branch_a, w_branch_b, w_out, ple_norm_g, w_ple_gate,
           w_ple_up, final_g):
    depth = w_in.shape[0]
    bsz, seqlen, _ = x.shape
    assert seqlen % (2 * STEP) == 0
    assert depth == 1
    bf16 = jnp.bfloat16
    o_z, o_xbc, o_dt = 0, D_INNER, D_INNER + D_CONV_CH
    o_u = o_dt + N_HEADS
    o_zp, o_ga, o_gb = o_u + D_POOL, o_u + 2 * D_POOL, o_u + 3 * D_POOL
    t = bsz * seqlen
    i = 0
    wi = w_in[i]
    w_main = jnp.concatenate([
        wi[:, o_xbc:o_xbc + D_CONV_CH], wi[:, o_u:o_u + D_POOL], wi[:, o_z:o_z + D_INNER],
        wi[:, o_zp:o_zp + D_POOL],
    ], axis=1).astype(bf16)
    w_dt = jnp.pad(wi[:, o_dt:o_dt + N_HEADS], ((0, 0), (0, LANES - N_HEADS))).astype(bf16)
    mixer_weights = (
        _row(norm_g[i]), w_main, w_dt,
        conv_w[i].astype(jnp.float32), _row(conv_b[i]), _row(dt_bias[i], LANES),
        _row(a_log[i], LANES), _row(jnp.repeat(d_skip[i], HEAD_DIM)), _row(gnorm_g[i]),
        pool_mix_w[i].astype(bf16), _row(pool_mix_b[i]), _row(pool_scale[i]),
    )
    tail_weights = (
        _row(norm_g[i]), wi[:, o_ga:o_ga + D_MODEL].astype(bf16),
        wi[:, o_gb:o_gb + D_MODEL].astype(bf16),
        w_branch_a[i].astype(bf16), w_branch_b[i].astype(bf16), w_out[i].astype(bf16),
        _row(ple_norm_g[i]), w_ple_gate[i].astype(bf16), w_ple_up[i].astype(bf16),
        _row(final_g),
    )
    x2d = x.reshape(t, D_MODEL)
    ya, yb = _mixer_call(x2d, mixer_weights, seqlen // STEP)
    out = _tail_call(x2d, p[i].reshape(t, PLE_DIM), ya, yb, tail_weights)
    return out.reshape(bsz, seqlen, D_MODEL)
```

```python
import functools
import math

import jax
import jax.numpy as jnp
from jax import lax
from jax.experimental import pallas as pl
from jax.experimental.pallas import tpu as pltpu

D_MODEL = 1024
PLE_DIM = 256
D_INNER = 2048
HEAD_DIM = 64
N_HEADS = 32
N_GROUPS = 4
HEADS_PER_GROUP = 8
D_STATE = 128
CONV_WIDTH = 4
D_CONV_CH = D_INNER + 2 * N_GROUPS * D_STATE
D_POOL = 1024
POOL_WINDOWS = (2, 4, 8, 16)
POOL_GROUP_DIM = 256
GROUP_WIDTH = D_INNER // N_GROUPS
EPS = 1e-6
LOG2E = math.log2(math.e)

LANES = 128
SUBLANES = 8
VMEM_BYTES_V7X = 64 * 1024 * 1024
VMEM_LIMIT_BYTES = VMEM_BYTES_V7X - 3 * 1024 * 1024

CHUNK = 128
STEP = 2 * CHUNK
CONV_TAIL = SUBLANES
POOL_TAIL = 3 * SUBLANES
CONV_SLABS = D_CONV_CH // LANES
POOL_SLABS = D_POOL // LANES
POOL_SLABS_PER_GROUP = POOL_GROUP_DIM // LANES

N_GATES = D_INNER + D_POOL
G_Z, G_ZP = 0, D_INNER
TAIL_ROWS = 512
PROJ_TN = 256
TAIL_TN = 256


def _sigmoid(v):
    return 0.5 + 0.5 * jnp.tanh(0.5 * v)


def _silu(v):
    h = 0.5 * v
    return h + h * jnp.tanh(h)


def _rms_scale(v):
    return v * lax.rsqrt(jnp.mean(v * v, axis=-1, keepdims=True) + EPS)


def _shift_rows(v, k):
    rows = lax.broadcasted_iota(jnp.int32, v.shape, 0)
    return jnp.where(rows >= k, pltpu.roll(v, k, axis=0), 0.0)


def _project(x, w, bufs):
    xpad_ref, upad_ref, gates_ref, dt_ref = bufs
    f32 = jnp.float32
    h = (_rms_scale(x) * w["norm_g"][...]).astype(jnp.bfloat16)

    def tile(w_ref, col):
        return jnp.dot(h, w_ref[:, col:col + PROJ_TN], preferred_element_type=f32)

    dt_ref[...] = jnp.dot(h, w["w_dt"][...], preferred_element_type=f32)
    for w_ref, width, dst_ref, row0 in ((w["w_xbc"], D_CONV_CH, xpad_ref, CONV_TAIL),
                                        (w["w_u"], D_POOL, upad_ref, POOL_TAIL)):
        for col in range(0, width, PROJ_TN):
            t = tile(w_ref, col)
            for k in range(PROJ_TN // LANES):
                dst_ref[col // LANES + k, row0:row0 + STEP, :] = t[:, k * LANES:(k + 1) * LANES]
    for w_ref, width, dst0 in ((w["w_z"], D_INNER, G_Z), (w["w_zp"], D_POOL, G_ZP)):
        for col in range(0, width, PROJ_TN):
            gates_ref[:, dst0 + col:dst0 + col + PROJ_TN] = tile(w_ref, col).astype(gates_ref.dtype)


def _conv_silu(convw_ref, convb_ref, xpad_ref, s, r0):
    cs = slice(s * LANES, (s + 1) * LANES)
    conv = convb_ref[:, cs]
    for k in range(CONV_WIDTH):
        off = r0 + CONV_TAIL - (CONV_WIDTH - 1) + k
        conv = conv + xpad_ref[s, off:off + CHUNK, :] * convw_ref[k:k + 1, cs]
    return _silu(conv)


def _pool_sums(upad_ref, upad_next_ref, tmp4_ref, tmp8_ref):
    t, h = STEP, POOL_TAIL

    def win(ref, s, start, n, w):
        acc = ref[s, start:start + n, :]
        for k in range(1, w):
            acc = acc + ref[s, start - k:start - k + n, :]
        return acc

    sums, centers = [], []
    for s in range(POOL_SLABS):
        w = POOL_WINDOWS[s // POOL_SLABS_PER_GROUP]
        centers.append(upad_ref[s, h:h + t, :])
        if w <= 4:
            sums.append(win(upad_ref, s, h, t, w))
        elif w == 8:
            i4 = s - 2 * POOL_SLABS_PER_GROUP
            tmp4_ref[i4, 8:16 + t, :] = win(upad_ref, s, h - 8, t + 8, 4)
            sums.append(tmp4_ref[i4, 16:16 + t, :] + tmp4_ref[i4, 12:12 + t, :])
        else:
            i4 = s - 2 * POOL_SLABS_PER_GROUP
            i8 = s - 3 * POOL_SLABS_PER_GROUP
            tmp4_ref[i4, 0:16 + t, :] = win(upad_ref, s, h - 16, t + 16, 4)
            tmp8_ref[i8, 0:8 + t, :] = tmp4_ref[i4, 8:16 + t, :] + tmp4_ref[i4, 4:12 + t, :]
            sums.append(tmp8_ref[i8, 8:8 + t, :] + tmp8_ref[i8, 0:t, :])
        upad_next_ref[s, 0:h, :] = upad_ref[s, t:t + h, :]
    return sums, centers


def _ssd_chunk(r0, dtv, xpad_ref, gates_ref, w, state_ref, ya_ref):
    f32, bf16 = jnp.float32, jnp.bfloat16
    L = CHUNK
    rs = slice(r0, r0 + L)
    conv = functools.partial(_conv_silu, w["conv_w"], w["conv_b"], xpad_ref)
    a_neg = -jnp.exp(w["a_log"][...]) * LOG2E
    head_lane = lax.broadcasted_iota(jnp.int32, (1, LANES), 1) < N_HEADS
    acs = jnp.where(head_lane, dtv * a_neg, 0.0)
    k = 1
    while k < L:
        acs = acs + _shift_rows(acs, k)
        k *= 2
    acs_t = acs.T

    rows = lax.broadcasted_iota(jnp.int32, (L, L), 0)
    cols = lax.broadcasted_iota(jnp.int32, (L, L), 1)
    causal = rows >= cols
    first_half = lax.broadcasted_iota(jnp.int32, (L, LANES), 1) < HEAD_DIM

    for g in range(N_GROUPS):
        b_g = conv(D_INNER // LANES + g, r0)
        c_g = conv(D_INNER // LANES + N_GROUPS + g, r0)
        b_bf, c_bf = b_g.astype(bf16), c_g.astype(bf16)
        cb = lax.dot_general(c_bf, b_bf, (((1,), (1,)), ((), ())), preferred_element_type=f32)
        cb = jnp.where(causal, cb, 0.0)
        h_prev = state_ref[g]
        y_off = jnp.dot(c_bf, h_prev.astype(bf16), preferred_element_type=f32)

        a_last_g, xw_g, y_g = [], [], []
        for j in range(HEADS_PER_GROUP // 2):
            ha = g * HEADS_PER_GROUP + 2 * j
            col0 = ha * HEAD_DIM
            lo = j * LANES
            col_a = jnp.broadcast_to(acs[:, ha:ha + 1], (L, L))
            col_b = jnp.broadcast_to(acs[:, ha + 1:ha + 2], (L, L))
            row_a = acs_t[ha:ha + 1, :]
            row_b = acs_t[ha + 1:ha + 2, :]
            m_a = (cb * jnp.exp2(jnp.minimum(col_a - row_a, 0.0))).astype(bf16)
            m_b = (cb * jnp.exp2(jnp.minimum(col_b - row_b, 0.0))).astype(bf16)

            a_exp = jnp.where(first_half, col_a, col_b)
            a_last = a_exp[L - 1:L, :]
            dt_exp = jnp.where(first_half, jnp.broadcast_to(dtv[:, ha:ha + 1], (L, LANES)),
                               jnp.broadcast_to(dtv[:, ha + 1:ha + 2], (L, LANES)))
            xs = conv(col0 // LANES, r0)
            xdt = xs * dt_exp
            xdt_bf = xdt.astype(bf16)
            xw_g.append((xdt * jnp.exp2(a_last - a_exp)).astype(bf16))

            zero = jnp.zeros_like(xdt_bf)
            rhs = jnp.concatenate([jnp.where(first_half, xdt_bf, zero),
                                   jnp.where(first_half, zero, xdt_bf)], axis=0)
            lhs = jnp.concatenate([m_a, m_b], axis=1)
            y = jnp.dot(lhs, rhs, preferred_element_type=f32)
            y = y + y_off[:, lo:lo + LANES] * jnp.exp2(a_exp)
            y = y + xs * w["d_skip"][:, col0:col0 + LANES]
            zz = gates_ref[rs, G_Z + col0:G_Z + col0 + LANES].astype(f32)
            y_g.append(y * _silu(zz))
            a_last_g.append(a_last)

        chunk_decay = jnp.exp2(jnp.concatenate(a_last_g, axis=1))
        upd = jnp.dot(b_g.T.astype(bf16), jnp.concatenate(xw_g, axis=1),
                      preferred_element_type=f32)
        state_ref[g] = h_prev * chunk_decay + upd

        gn = w["gnorm_g"][:, g * GROUP_WIDTH:(g + 1) * GROUP_WIDTH]
        yg = _rms_scale(jnp.concatenate(y_g, axis=1)) * gn
        ya_ref[rs, g * GROUP_WIDTH:(g + 1) * GROUP_WIDTH] = yg.astype(bf16)


def _mix(c, cur, nxt, w, ya_ref, yb_ref, scr):
    f32, bf16 = jnp.float32, jnp.bfloat16
    xpad_ref, upad_ref, gates_ref, dt_ref = cur
    tmp4_ref, tmp8_ref, state_ref = scr

    dtv = dt_ref[...] + w["dt_bias"][...]
    dtv = jnp.maximum(dtv, 0.0) + jnp.log1p(jnp.exp(-jnp.abs(dtv)))
    for q in range(STEP // CHUNK):
        r0 = q * CHUNK
        _ssd_chunk(r0, dtv[r0:r0 + CHUNK], xpad_ref, gates_ref, w, state_ref, ya_ref)
    for s in range(CONV_SLABS):
        nxt[0][s, 0:CONV_TAIL, :] = xpad_ref[s, STEP:STEP + CONV_TAIL, :]

    sums, centers = _pool_sums(upad_ref, nxt[1], tmp4_ref, tmp8_ref)
    pos = lax.broadcasted_iota(jnp.int32, (STEP, LANES), 0) + (c * STEP + 1)
    for gi, win in enumerate(POOL_WINDOWS):
        gs = slice(gi * POOL_GROUP_DIM, (gi + 1) * POOL_GROUP_DIM)
        cnt = jnp.minimum(pos, win).astype(f32)
        pooled = [(sums[s] / cnt - centers[s]).astype(bf16)
                  for s in range(gi * POOL_SLABS_PER_GROUP, (gi + 1) * POOL_SLABS_PER_GROUP)]
        mixed = jnp.dot(jnp.concatenate(pooled, axis=1), w["mix_w"][gi], preferred_element_type=f32)
        yb = (mixed + w["mix_b"][:, gs]) * w["pool_scale"][:, gs]
        zp = gates_ref[:, G_ZP + gi * POOL_GROUP_DIM:G_ZP + (gi + 1) * POOL_GROUP_DIM].astype(f32)
        yb_ref[:, gs] = (yb * _silu(zp)).astype(bf16)


MIXER_WEIGHTS = ("norm_g", "w_xbc", "w_u", "w_z", "w_zp", "w_dt", "conv_w", "conv_b", "dt_bias", "a_log", "d_skip",
                 "gnorm_g", "mix_w", "mix_b", "pool_scale")


def _mixer_kernel(steps_per_seq, x_ref, xn_ref, *refs):
    nw = len(MIXER_WEIGHTS)
    w = dict(zip(MIXER_WEIGHTS, refs[:nw]))
    ya_ref, yb_ref = refs[nw:nw + 2]
    xpad0, upad0, gates0, dt0, xpad1, upad1, gates1, dt1 = refs[nw + 2:nw + 10]
    scr = refs[nw + 10:]
    state_ref = scr[2]
    sets = ((xpad0, upad0, gates0, dt0), (xpad1, upad1, gates1, dt1))
    i = pl.program_id(0)
    c = lax.rem(i, steps_per_seq)

    @pl.when(i == 0)
    def _():
        _project(x_ref[...], w, sets[0])

    @pl.when(c == 0)
    def _():
        xpad0[:, 0:CONV_TAIL, :] = jnp.zeros((CONV_SLABS, CONV_TAIL, LANES), jnp.float32)
        upad0[:, 0:POOL_TAIL, :] = jnp.zeros((POOL_SLABS, POOL_TAIL, LANES), jnp.float32)
        state_ref[...] = jnp.zeros(state_ref.shape, jnp.float32)

    for parity in range(2):
        @pl.when(lax.rem(i, 2) == parity)
        def _():
            cur, nxt = sets[parity], sets[1 - parity]
            _project(xn_ref[...], w, nxt)
            _mix(c, cur, nxt, w, ya_ref, yb_ref, scr)


def _const_spec(a):
    return pl.BlockSpec(a.shape, lambda i: (0,) * a.ndim, pipeline_mode=pl.Buffered(1))


def _mixer_call(x2d, weights, steps_per_seq):
    t = x2d.shape[0]
    n = t // STEP
    assert steps_per_seq % 2 == 0
    buf_set = [
        pltpu.VMEM((CONV_SLABS, CONV_TAIL + STEP, LANES), jnp.float32),
        pltpu.VMEM((POOL_SLABS, POOL_TAIL + STEP, LANES), jnp.float32),
        pltpu.VMEM((STEP, N_GATES), jnp.bfloat16),
        pltpu.VMEM((STEP, LANES), jnp.float32),
    ]
    return pl.pallas_call(
        functools.partial(_mixer_kernel, steps_per_seq),
        grid=(n,),
        in_specs=[
            pl.BlockSpec((STEP, D_MODEL), lambda i: (0, 0)),
            pl.BlockSpec((STEP, D_MODEL), lambda i: (jnp.minimum(i + 1, n - 1), 0)),
        ] + [_const_spec(a) for a in weights],
        out_specs=[
            pl.BlockSpec((STEP, D_INNER), lambda i: (i, 0)),
            pl.BlockSpec((STEP, D_POOL), lambda i: (i, 0)),
        ],
        out_shape=[
            jax.ShapeDtypeStruct((t, D_INNER), jnp.bfloat16),
            jax.ShapeDtypeStruct((t, D_POOL), jnp.bfloat16),
        ],
        scratch_shapes=buf_set + buf_set + [
            pltpu.VMEM((2 * POOL_SLABS_PER_GROUP, 16 + STEP, LANES), jnp.float32),
            pltpu.VMEM((POOL_SLABS_PER_GROUP, 8 + STEP, LANES), jnp.float32),
            pltpu.VMEM((N_GROUPS, D_STATE, GROUP_WIDTH), jnp.float32),
        ],
        compiler_params=pltpu.CompilerParams(
            dimension_semantics=("arbitrary",),
            vmem_limit_bytes=VMEM_LIMIT_BYTES),
        name="mixer",
    )(x2d, x2d, *weights)


TAIL_WEIGHTS = ("norm_g", "w_ga", "w_gb", "w_a", "w_b", "w_out", "ple_g", "w_gate", "w_up",
                "final_g")


def _tail_kernel(x_ref, p_ref, ya_ref, yb_ref, *refs):
    nw = len(TAIL_WEIGHTS)
    w = dict(zip(TAIL_WEIGHTS, refs[:nw]))
    out_ref, h_ref, merged_ref, hg_ref = refs[nw:]
    f32, bf16 = jnp.float32, jnp.bfloat16
    rows = x_ref.shape[0]
    tiles = [slice(n * TAIL_TN, (n + 1) * TAIL_TN) for n in range(D_MODEL // TAIL_TN)]

    h_ref[...] = (_rms_scale(x_ref[...]) * w["norm_g"][...]).astype(bf16)
    for ts in tiles:
        g_a = jnp.dot(h_ref[...], w["w_ga"][:, ts], preferred_element_type=f32)
        g_b = jnp.dot(h_ref[...], w["w_gb"][:, ts], preferred_element_type=f32)
        proj_a = jnp.dot(ya_ref[...], w["w_a"][:, ts], preferred_element_type=f32)
        proj_b = jnp.dot(yb_ref[...], w["w_b"][:, ts], preferred_element_type=f32)
        merged_ref[:, ts] = (_sigmoid(g_a) * proj_a + _sigmoid(g_b) * proj_b).astype(bf16)
    ssq = jnp.zeros((rows, 1), f32)
    for ts in tiles:
        x1 = x_ref[:, ts] + jnp.dot(merged_ref[...], w["w_out"][:, ts], preferred_element_type=f32)
        out_ref[:, ts] = x1
        ssq = ssq + jnp.sum(x1 * x1, axis=-1, keepdims=True)

    inv = lax.rsqrt(ssq * (1.0 / D_MODEL) + EPS)
    for ts in tiles:
        hg_ref[:, ts] = (out_ref[:, ts] * inv * w["ple_g"][:, ts]).astype(bf16)
    p_bf = p_ref[...].astype(bf16)
    ssq = jnp.zeros((rows, 1), f32)
    for ts in tiles:
        gate = _sigmoid(jnp.dot(hg_ref[...], w["w_gate"][:, ts], preferred_element_type=f32))
        up = jnp.dot(p_bf, w["w_up"][:, ts], preferred_element_type=f32)
        x2 = out_ref[:, ts] + gate * up
        out_ref[:, ts] = x2
        ssq = ssq + jnp.sum(x2 * x2, axis=-1, keepdims=True)
    inv = lax.rsqrt(ssq * (1.0 / D_MODEL) + EPS)
    for ts in tiles:
        out_ref[:, ts] = out_ref[:, ts] * inv * w["final_g"][:, ts]


def _tail_call(x2d, p2d, ya, yb, weights):
    t = x2d.shape[0]
    tb = TAIL_ROWS if t % TAIL_ROWS == 0 else STEP
    row_spec = lambda width: pl.BlockSpec((tb, width), lambda i: (i, 0))
    return pl.pallas_call(
        _tail_kernel,
        grid=(t // tb,),
        in_specs=[row_spec(D_MODEL), row_spec(PLE_DIM), row_spec(D_INNER), row_spec(D_POOL)]
        + [_const_spec(a) for a in weights],
        out_specs=row_spec(D_MODEL),
        out_shape=jax.ShapeDtypeStruct((t, D_MODEL), jnp.float32),
        scratch_shapes=[
            pltpu.VMEM((tb, D_MODEL), jnp.bfloat16),
            pltpu.VMEM((tb, D_MODEL), jnp.bfloat16),
            pltpu.VMEM((tb, D_MODEL), jnp.bfloat16),
        ],
        compiler_params=pltpu.CompilerParams(
            dimension_semantics=("arbitrary",),
            vmem_limit_bytes=VMEM_LIMIT_BYTES),
        name="tail",
    )(x2d, p2d, ya, yb, *weights)


def _row(v, width=None):
    v = v.reshape(1, -1).astype(jnp.float32)
    if width is not None and v.shape[1] < width:
        v = jnp.pad(v, ((0, 0), (0, width - v.shape[1])))
    return v


def kernel(x, p, norm_g, w_in, conv_w, conv_b, dt_bias, a_log, d_skip, gnorm_g, pool_mix_w,
           pool_mix_b, pool_scale, w_branch_a, w_branch_b, w_out, ple_norm_g, w_ple_gate,
           w_ple_up, final_g):
    depth = w_in.shape[0]
    bsz, seqlen, _ = x.shape
    assert seqlen % (2 * STEP) == 0
    assert depth == 1
    bf16 = jnp.bfloat16
    o_z, o_xbc, o_dt = 0, D_INNER, D_INNER + D_CONV_CH
    o_u = o_dt + N_HEADS
    o_zp, o_ga, o_gb = o_u + D_POOL, o_u + 2 * D_POOL, o_u + 3 * D_POOL
    t = bsz * seqlen
    i = 0
    wi = w_in[i]
    w_dt = jnp.pad(wi[:, o_dt:o_dt + N_HEADS], ((0, 0), (0, LANES - N_HEADS))).astype(bf16)
    mixer_weights = (
        _row(norm_g[i]), wi[:, o_xbc:o_xbc + D_CONV_CH].astype(bf16),
        wi[:, o_u:o_u + D_POOL].astype(bf16), wi[:, o_z:o_z + D_INNER].astype(bf16),
        wi[:, o_zp:o_zp + D_POOL].astype(bf16), w_dt,
        conv_w[i].astype(jnp.float32), _row(conv_b[i]), _row(dt_bias[i], LANES),
        _row(a_log[i], LANES), _row(jnp.repeat(d_skip[i], HEAD_DIM)), _row(gnorm_g[i]),
        pool_mix_w[i].astype(bf16), _row(pool_mix_b[i]), _row(pool_scale[i]),
    )
    tail_weights = (
        _row(norm_g[i]), wi[:, o_ga:o_ga + D_MODEL].astype(bf16),
        wi[:, o_gb:o_gb + D_MODEL].astype(bf16),
        w_branch_a[i].astype(bf16), w_branch_b[i].astype(bf16), w_out[i].astype(bf16),
        _row(ple_norm_g[i]), w_ple_gate[i].astype(bf16), w_ple_up[i].astype(bf16),
        _row(final_g),
    )
    x2d = x.reshape(t, D_MODEL)
    ya, yb = _mixer_call(x2d, mixer_weights, seqlen // STEP)
    out = _tail_call(x2d, p[i].reshape(t, PLE_DIM), ya, yb, tail_weights)
    return out.reshape(bsz, seqlen, D_MODEL)
```

```python
import functools
import math

import jax
import jax.numpy as jnp
from jax import lax
from jax.experimental import pallas as pl
from jax.experimental.pallas import tpu as pltpu

D_MODEL = 1024
PLE_DIM = 256
D_INNER = 2048
HEAD_DIM = 64
N_HEADS = 32
N_GROUPS = 4
HEADS_PER_GROUP = 8
D_STATE = 128
CONV_WIDTH = 4
D_CONV_CH = D_INNER + 2 * N_GROUPS * D_STATE
D_POOL = 1024
POOL_WINDOWS = (2, 4, 8, 16)
POOL_GROUP_DIM = 256
GROUP_WIDTH = D_INNER // N_GROUPS
EPS = 1e-6
LOG2E = math.log2(math.e)

LANES = 128
SUBLANES = 8
VMEM_BYTES_V7X = 64 * 1024 * 1024
VMEM_LIMIT_BYTES = VMEM_BYTES_V7X - 3 * 1024 * 1024

CHUNK = 128
STEP = 2 * CHUNK
CONV_TAIL = SUBLANES
POOL_TAIL = 3 * SUBLANES
CONV_SLABS = D_CONV_CH // LANES
POOL_SLABS = D_POOL // LANES
POOL_SLABS_PER_GROUP = POOL_GROUP_DIM // LANES

N_GATES = D_INNER + D_POOL
G_Z, G_ZP = 0, D_INNER
TAIL_ROWS = 512
PROJ_TN = 256
TAIL_TN = 256


def _sigmoid(v):
    return 0.5 + 0.5 * jnp.tanh(0.5 * v)


def _silu(v):
    h = 0.5 * v
    return h + h * jnp.tanh(h)


def _rms_scale(v):
    return v * lax.rsqrt(jnp.mean(v * v, axis=-1, keepdims=True) + EPS)


def _shift_rows(v, k):
    rows = lax.broadcasted_iota(jnp.int32, v.shape, 0)
    return jnp.where(rows >= k, pltpu.roll(v, k, axis=0), 0.0)


def _project(x, w, bufs):
    xpad_ref, upad_ref, gates_ref, dt_ref = bufs
    f32 = jnp.float32
    h = (_rms_scale(x) * w["norm_g"][...]).astype(jnp.bfloat16)

    def tile(w_ref, col):
        return jnp.dot(h, w_ref[:, col:col + PROJ_TN], preferred_element_type=f32)

    dt_ref[...] = jnp.dot(h, w["w_dt"][...], preferred_element_type=f32)
    for w_ref, width, dst_ref, row0 in ((w["w_xbc"], D_CONV_CH, xpad_ref, CONV_TAIL),
                                        (w["w_u"], D_POOL, upad_ref, POOL_TAIL)):
        for col in range(0, width, PROJ_TN):
            t = tile(w_ref, col)
            for k in range(PROJ_TN // LANES):
                dst_ref[col // LANES + k, row0:row0 + STEP, :] = t[:, k * LANES:(k + 1) * LANES]
    for w_ref, width, dst0 in ((w["w_z"], D_INNER, G_Z), (w["w_zp"], D_POOL, G_ZP)):
        for col in range(0, width, PROJ_TN):
            gates_ref[:, dst0 + col:dst0 + col + PROJ_TN] = tile(w_ref, col).astype(gates_ref.dtype)


def _conv_silu(convw_ref, convb_ref, xpad_ref, s, r0):
    cs = slice(s * LANES, (s + 1) * LANES)
    conv = convb_ref[:, cs]
    for k in range(CONV_WIDTH):
        off = r0 + CONV_TAIL - (CONV_WIDTH - 1) + k
        conv = conv + xpad_ref[s, off:off + CHUNK, :] * convw_ref[k:k + 1, cs]
    return _silu(conv)


def _pool_sums(upad_ref, upad_next_ref, tmp4_ref, tmp8_ref):
    t, h = STEP, POOL_TAIL

    def win(ref, s, start, n, w):
        acc = ref[s, start:start + n, :]
        for k in range(1, w):
            acc = acc + ref[s, start - k:start - k + n, :]
        return acc

    sums, centers = [], []
    for s in range(POOL_SLABS):
        w = POOL_WINDOWS[s // POOL_SLABS_PER_GROUP]
        centers.append(upad_ref[s, h:h + t, :])
        if w <= 4:
            sums.append(win(upad_ref, s, h, t, w))
        elif w == 8:
            i4 = s - 2 * POOL_SLABS_PER_GROUP
            tmp4_ref[i4, 8:16 + t, :] = win(upad_ref, s, h - 8, t + 8, 4)
            sums.append(tmp4_ref[i4, 16:16 + t, :] + tmp4_ref[i4, 12:12 + t, :])
        else:
            i4 = s - 2 * POOL_SLABS_PER_GROUP
            i8 = s - 3 * POOL_SLABS_PER_GROUP
            tmp4_ref[i4, 0:16 + t, :] = win(upad_ref, s, h - 16, t + 16, 4)
            tmp8_ref[i8, 0:8 + t, :] = tmp4_ref[i4, 8:16 + t, :] + tmp4_ref[i4, 4:12 + t, :]
            sums.append(tmp8_ref[i8, 8:8 + t, :] + tmp8_ref[i8, 0:t, :])
        upad_next_ref[s, 0:h, :] = upad_ref[s, t:t + h, :]
    return sums, centers


def _ssd_chunk(r0, dtv, xpad_ref, gates_ref, w, state_ref, ya_ref):
    f32, bf16 = jnp.float32, jnp.bfloat16
    L = CHUNK
    rs = slice(r0, r0 + L)
    conv = functools.partial(_conv_silu, w["conv_w"], w["conv_b"], xpad_ref)
    a_neg = -jnp.exp(w["a_log"][...]) * LOG2E
    head_lane = lax.broadcasted_iota(jnp.int32, (1, LANES), 1) < N_HEADS
    acs = jnp.where(head_lane, dtv * a_neg, 0.0)
    k = 1
    while k < L:
        acs = acs + _shift_rows(acs, k)
        k *= 2
    acs_t = acs.T

    rows = lax.broadcasted_iota(jnp.int32, (L, L), 0)
    cols = lax.broadcasted_iota(jnp.int32, (L, L), 1)
    causal = rows >= cols
    first_half = lax.broadcasted_iota(jnp.int32, (L, LANES), 1) < HEAD_DIM

    for g in range(N_GROUPS):
        b_g = conv(D_INNER // LANES + g, r0)
        c_g = conv(D_INNER // LANES + N_GROUPS + g, r0)
        b_bf, c_bf = b_g.astype(bf16), c_g.astype(bf16)
        cb = lax.dot_general(c_bf, b_bf, (((1,), (1,)), ((), ())), preferred_element_type=f32)
        cb = jnp.where(causal, cb, 0.0)
        h_prev = state_ref[g]
        y_off = jnp.dot(c_bf, h_prev.astype(bf16), preferred_element_type=f32)

        a_last_g, xw_g, y_g = [], [], []
        for j in range(HEADS_PER_GROUP // 2):
            ha = g * HEADS_PER_GROUP + 2 * j
            col0 = ha * HEAD_DIM
            lo = j * LANES
            col_a = jnp.broadcast_to(acs[:, ha:ha + 1], (L, L))
            col_b = jnp.broadcast_to(acs[:, ha + 1:ha + 2], (L, L))
            row_a = acs_t[ha:ha + 1, :]
            row_b = acs_t[ha + 1:ha + 2, :]
            m_a = (cb * jnp.exp2(jnp.minimum(col_a - row_a, 0.0))).astype(bf16)
            m_b = (cb * jnp.exp2(jnp.minimum(col_b - row_b, 0.0))).astype(bf16)

            a_exp = jnp.where(first_half, col_a, col_b)
            a_last = a_exp[L - 1:L, :]
            dt_exp = jnp.where(first_half, jnp.broadcast_to(dtv[:, ha:ha + 1], (L, LANES)),
                               jnp.broadcast_to(dtv[:, ha + 1:ha + 2], (L, LANES)))
            xs = conv(col0 // LANES, r0)
            xdt = xs * dt_exp
            xdt_bf = xdt.astype(bf16)
            xw_g.append((xdt * jnp.exp2(a_last - a_exp)).astype(bf16))

            zero = jnp.zeros_like(xdt_bf)
            rhs = jnp.concatenate([jnp.where(first_half, xdt_bf, zero),
                                   jnp.where(first_half, zero, xdt_bf)], axis=0)
            lhs = jnp.concatenate([m_a, m_b], axis=1)
            y = jnp.dot(lhs, rhs, preferred_element_type=f32)
            y = y + y_off[:, lo:lo + LANES] * jnp.exp2(a_exp)
            y = y + xs * w["d_skip"][:, col0:col0 + LANES]
            zz = gates_ref[rs, G_Z + col0:G_Z + col0 + LANES].astype(f32)
            y_g.append(y * _silu(zz))
            a_last_g.append(a_last)

        chunk_decay = jnp.exp2(jnp.concatenate(a_last_g, axis=1))
        upd = jnp.dot(b_g.T.astype(bf16), jnp.concatenate(xw_g, axis=1),
                      preferred_element_type=f32)
        state_ref[g] = h_prev * chunk_decay + upd

        gn = w["gnorm_g"][:, g * GROUP_WIDTH:(g + 1) * GROUP_WIDTH]
        yg = _rms_scale(jnp.concatenate(y_g, axis=1)) * gn
        ya_ref[rs, g * GROUP_WIDTH:(g + 1) * GROUP_WIDTH] = yg.astype(bf16)


def _mix(c, cur, nxt, w, ya_ref, yb_ref, scr):
    f32, bf16 = jnp.float32, jnp.bfloat16
    xpad_ref, upad_ref, gates_ref, dt_ref = cur
    tmp4_ref, tmp8_ref, state_ref = scr

    dtv = dt_ref[...] + w["dt_bias"][...]
    dtv = jnp.maximum(dtv, 0.0) + jnp.log1p(jnp.exp(-jnp.abs(dtv)))
    for q in range(STEP // CHUNK):
        r0 = q * CHUNK
        _ssd_chunk(r0, dtv[r0:r0 + CHUNK], xpad_ref, gates_ref, w, state_ref, ya_ref)
    for s in range(CONV_SLABS):
        nxt[0][s, 0:CONV_TAIL, :] = xpad_ref[s, STEP:STEP + CONV_TAIL, :]

    sums, centers = _pool_sums(upad_ref, nxt[1], tmp4_ref, tmp8_ref)
    pos = lax.broadcasted_iota(jnp.int32, (STEP, LANES), 0) + (c * STEP + 1)
    for gi, win in enumerate(POOL_WINDOWS):
        gs = slice(gi * POOL_GROUP_DIM, (gi + 1) * POOL_GROUP_DIM)
        cnt = jnp.minimum(pos, win).astype(f32)
        pooled = [(sums[s] / cnt - centers[s]).astype(bf16)
                  for s in range(gi * POOL_SLABS_PER_GROUP, (gi + 1) * POOL_SLABS_PER_GROUP)]
        mixed = jnp.dot(jnp.concatenate(pooled, axis=1), w["mix_w"][gi], preferred_element_type=f32)
        yb = (mixed + w["mix_b"][:, gs]) * w["pool_scale"][:, gs]
        zp = gates_ref[:, G_ZP + gi * POOL_GROUP_DIM:G_ZP + (gi + 1) * POOL_GROUP_DIM].astype(f32)
        yb_ref[:, gs] = (yb * _silu(zp)).astype(bf16)


MIXER_WEIGHTS = ("norm_g", "w_xbc", "w_u", "w_z", "w_zp", "w_dt", "conv_w", "conv_b", "dt_bias", "a_log", "d_skip",
                 "gnorm_g", "mix_w", "mix_b", "pool_scale")


def _mixer_kernel(steps_per_seq, x_ref, xn_ref, *refs):
    nw = len(MIXER_WEIGHTS)
    w = dict(zip(MIXER_WEIGHTS, refs[:nw]))
    ya_ref, yb_ref = refs[nw:nw + 2]
    xpad0, upad0, gates0, dt0, xpad1, upad1, gates1, dt1 = refs[nw + 2:nw + 10]
    scr = refs[nw + 10:]
    state_ref = scr[2]
    sets = ((xpad0, upad0, gates0, dt0), (xpad1, upad1, gates1, dt1))
    i = pl.program_id(0)
    c = lax.rem(i, steps_per_seq)

    @pl.when(i == 0)
    def _():
        _project(x_ref[...], w, sets[0])

    @pl.when(c == 0)
    def _():
        xpad0[:, 0:CONV_TAIL, :] = jnp.zeros((CONV_SLABS, CONV_TAIL, LANES), jnp.float32)
        upad0[:, 0:POOL_TAIL, :] = jnp.zeros((POOL_SLABS, POOL_TAIL, LANES), jnp.float32)
        state_ref[...] = jnp.zeros(state_ref.shape, jnp.float32)

    for parity in range(2):
        @pl.when(lax.rem(i, 2) == parity)
        def _():
            cur, nxt = sets[parity], sets[1 - parity]
            _project(xn_ref[...], w, nxt)
            _mix(c, cur, nxt, w, ya_ref, yb_ref, scr)


def _const_spec(a):
    return pl.BlockSpec(a.shape, lambda i: (0,) * a.ndim, pipeline_mode=pl.Buffered(1))


def _mixer_call(x2d, weights, steps_per_seq):
    t = x2d.shape[0]
    n = t // STEP
    assert steps_per_seq % 2 == 0
    buf_set = [
        pltpu.VMEM((CONV_SLABS, CONV_TAIL + STEP, LANES), jnp.float32),
        pltpu.VMEM((POOL_SLABS, POOL_TAIL + STEP, LANES), jnp.float32),
        pltpu.VMEM((STEP, N_GATES), jnp.bfloat16),
        pltpu.VMEM((STEP, LANES), jnp.float32),
    ]
    return pl.pallas_call(
        functools.partial(_mixer_kernel, steps_per_seq),
        grid=(n,),
        in_specs=[
            pl.BlockSpec((STEP, D_MODEL), lambda i: (0, 0)),
            pl.BlockSpec((STEP, D_MODEL), lambda i: (jnp.minimum(i + 1, n - 1), 0)),
        ] + [_const_spec(a) for a in weights],
        out_specs=[
            pl.BlockSpec((STEP, D_INNER), lambda i: (i, 0)),
            pl.BlockSpec((STEP, D_POOL), lambda i: (i, 0)),
        ],
        out_shape=[
            jax.ShapeDtypeStruct((t, D_INNER), jnp.bfloat16),
            jax.ShapeDtypeStruct((t, D_POOL), jnp.bfloat16),
        ],
        scratch_shapes=buf_set + buf_set + [
            pltpu.VMEM((2 * POOL_SLABS_PER_GROUP, 16 + STEP, LANES), jnp.float32),
            pltpu.VMEM((POOL_SLABS_PER_GROUP, 8 + STEP, LANES), jnp.float32),
            pltpu.VMEM((N_GROUPS, D_STATE, GROUP_WIDTH), jnp.float32),
        ],
        compiler_params=pltpu.CompilerParams(
            dimension_semantics=("arbitrary",),
            vmem_limit_bytes=VMEM_LIMIT_BYTES),
        name="mixer",
    )(x2d, x2d, *weights)


TAIL_WEIGHTS = ("norm_g", "w_ga", "w_gb", "w_a", "w_b", "w_out", "ple_g", "w_gate", "w_up",
                "final_g")


def _tail_kernel(x_ref, p_ref, ya_ref, yb_ref, *refs):
    nw = len(TAIL_WEIGHTS)
    w = dict(zip(TAIL_WEIGHTS, refs[:nw]))
    out_ref, h_ref, merged_ref, hg_ref = refs[nw:]
    f32, bf16 = jnp.float32, jnp.bfloat16
    rows = x_ref.shape[0]
    tiles = [slice(n * TAIL_TN, (n + 1) * TAIL_TN) for n in range(D_MODEL // TAIL_TN)]

    h_ref[...] = (_rms_scale(x_ref[...]) * w["norm_g"][...]).astype(bf16)
    for ts in tiles:
        g_a = jnp.dot(h_ref[...], w["w_ga"][:, ts], preferred_element_type=f32)
        g_b = jnp.dot(h_ref[...], w["w_gb"][:, ts], preferred_element_type=f32)
        proj_a = jnp.dot(ya_ref[...], w["w_a"][:, ts], preferred_element_type=f32)
        proj_b = jnp.dot(yb_ref[...], w["w_b"][:, ts], preferred_element_type=f32)
        merged_ref[:, ts] = (_sigmoid(g_a) * proj_a + _sigmoid(g_b) * proj_b).astype(bf16)
    ssq = jnp.zeros((rows, 1), f32)
    for ts in tiles:
        x1 = x_ref[:, ts] + jnp.dot(merged_ref[...], w["w_out"][:, ts], preferred_element_type=f32)
        out_ref[:, ts] = x1
        ssq = ssq + jnp.sum(x1 * x1, axis=-1, keepdims=True)

    inv = lax.rsqrt(ssq * (1.0 / D_MODEL) + EPS)
    for ts in tiles:
        hg_ref[:, ts] = (out_ref[:, ts] * inv * w["ple_g"][:, ts]).astype(bf16)
    p_bf = p_ref[...].astype(bf16)
    ssq = jnp.zeros((rows, 1), f32)
    for ts in tiles:
        gate = _sigmoid(jnp.dot(hg_ref[...], w["w_gate"][:, ts], preferred_element_type=f32))
        up = jnp.dot(p_bf, w["w_up"][:, ts], preferred_element_type=f32)
        x2 = out_ref[:, ts] + gate * up
        out_ref[:, ts] = x2
        ssq = ssq + jnp.sum(x2 * x2, axis=-1, keepdims=True)
    inv = lax.rsqrt(ssq * (1.0 / D_MODEL) + EPS)
    for ts in tiles:
        out_ref[:, ts] = out_ref[:, ts] * inv * w["final_g"][:, ts]


def _tail_call(x2d, p2d, ya, yb, weights):
    t = x2d.shape[0]
    tb = TAIL_ROWS if t % TAIL_ROWS == 0 else STEP
    row_spec = lambda width: pl.BlockSpec((tb, width), lambda i: (i, 0))
    return pl.pallas_call(
        _tail_kernel,
        grid=(t // tb,),
        in_specs=[row_spec(D_MODEL), row_spec(PLE_DIM), row_spec(D_INNER), row_spec(D_POOL)]
        + [_const_spec(a) for a in weights],
        out_specs=row_spec(D_MODEL),
        out_shape=jax.ShapeDtypeStruct((t, D_MODEL), jnp.float32),
        scratch_shapes=[
            pltpu.VMEM((tb, D_MODEL), jnp.bfloat16),
            pltpu.VMEM((tb, D_MODEL), jnp.bfloat16),
            pltpu.VMEM((tb, D_MODEL), jnp.bfloat16),
        ],
        compiler_params=pltpu.CompilerParams(
            dimension_semantics=("arbitrary",),
            vmem_limit_bytes=VMEM_LIMIT_BYTES),
        name="tail",
    )(x2d, p2d, ya, yb, *weights)


def _row(v, width=None):
    v = v.reshape(1, -1).astype(jnp.float32)
    if width is not None and v.shape[1] < width:
        v = jnp.pad(v, ((0, 0), (0, width - v.shape[1])))
    return v


def kernel(x, p, norm_g, w_in, conv_w, conv_b, dt_bias, a_log, d_skip, gnorm_g, pool_mix_w,
           pool_mix_b, pool_scale, w_branch_a, w_branch_b, w_out, ple_norm_g, w_ple_gate,
           w_ple_up, final_g):
    depth = w_in.shape[0]
    bsz, seqlen, _ = x.shape
    assert seqlen % (2 * STEP) == 0
    assert depth == 1
    bf16 = jnp.bfloat16
    o_z, o_xbc, o_dt = 0, D_INNER, D_INNER + D_CONV_CH
    o_u = o_dt + N_HEADS
    o_zp, o_ga, o_gb = o_u + D_POOL, o_u + 2 * D_POOL, o_u + 3 * D_POOL
    t = bsz * seqlen
    i = 0
    wi = w_in.reshape(D_MODEL, -1)
    w_dt = jnp.pad(wi[:, o_dt:o_dt + N_HEADS], ((0, 0), (0, LANES - N_HEADS))).astype(bf16)
    mixer_weights = (
        _row(norm_g[i]), wi[:, o_xbc:o_xbc + D_CONV_CH].astype(bf16),
        wi[:, o_u:o_u + D_POOL].astype(bf16), wi[:, o_z:o_z + D_INNER].astype(bf16),
        wi[:, o_zp:o_zp + D_POOL].astype(bf16), w_dt,
        conv_w[i].astype(jnp.float32), _row(conv_b[i]), _row(dt_bias[i], LANES),
        _row(a_log[i], LANES), _row(jnp.repeat(d_skip[i], HEAD_DIM)), _row(gnorm_g[i]),
        pool_mix_w[i].astype(bf16), _row(pool_mix_b[i]), _row(pool_scale[i]),
    )
    tail_weights = (
        _row(norm_g[i]), wi[:, o_ga:o_ga + D_MODEL].astype(bf16),
        wi[:, o_gb:o_gb + D_MODEL].astype(bf16),
        w_branch_a[i].astype(bf16), w_branch_b[i].astype(bf16), w_out[i].astype(bf16),
        _row(ple_norm_g[i]), w_ple_gate[i].astype(bf16), w_ple_up[i].astype(bf16),
        _row(final_g),
    )
    x2d = x.reshape(t, D_MODEL)
    ya, yb = _mixer_call(x2d, mixer_weights, seqlen // STEP)
    out = _tail_call(x2d, p.reshape(depth * t, PLE_DIM), ya, yb, tail_weights)
    return out.reshape(bsz, seqlen, D_MODEL)
```

```python
import functools
import math

import jax
import jax.numpy as jnp
from jax import lax
from jax.experimental import pallas as pl
from jax.experimental.pallas import tpu as pltpu

D_MODEL = 1024
PLE_DIM = 256
D_INNER = 2048
HEAD_DIM = 64
N_HEADS = 32
N_GROUPS = 4
HEADS_PER_GROUP = 8
D_STATE = 128
CONV_WIDTH = 4
D_CONV_CH = D_INNER + 2 * N_GROUPS * D_STATE
D_POOL = 1024
POOL_WINDOWS = (2, 4, 8, 16)
POOL_GROUP_DIM = 256
GROUP_WIDTH = D_INNER // N_GROUPS
EPS = 1e-6
LOG2E = math.log2(math.e)

LANES = 128
SUBLANES = 8
VMEM_BYTES_V7X = 64 * 1024 * 1024
VMEM_LIMIT_BYTES = VMEM_BYTES_V7X - 3 * 1024 * 1024

CHUNK = 128
STEP = 2 * CHUNK
CONV_TAIL = SUBLANES
POOL_TAIL = 3 * SUBLANES
CONV_SLABS = D_CONV_CH // LANES
POOL_SLABS = D_POOL // LANES
POOL_SLABS_PER_GROUP = POOL_GROUP_DIM // LANES

N_GATES = D_INNER + D_POOL
G_Z, G_ZP = 0, D_INNER
TAIL_ROWS = 1024
PROJ_TN = 256
TAIL_TN = 256


def _sigmoid(v):
    return 0.5 + 0.5 * jnp.tanh(0.5 * v)


def _silu(v):
    h = 0.5 * v
    return h + h * jnp.tanh(h)


def _rms_scale(v):
    return v * lax.rsqrt(jnp.mean(v * v, axis=-1, keepdims=True) + EPS)


def _shift_rows(v, k):
    rows = lax.broadcasted_iota(jnp.int32, v.shape, 0)
    return jnp.where(rows >= k, pltpu.roll(v, k, axis=0), 0.0)


def _project(x, w, bufs):
    xpad_ref, upad_ref, gates_ref, dt_ref = bufs
    f32 = jnp.float32
    h = (_rms_scale(x) * w["norm_g"][...]).astype(jnp.bfloat16)

    def tile(w_ref, col):
        return jnp.dot(h, w_ref[:, col:col + PROJ_TN], preferred_element_type=f32)

    dt_ref[...] = jnp.dot(h, w["w_dt"][...], preferred_element_type=f32)
    for w_ref, width, dst_ref, row0 in ((w["w_xbc"], D_CONV_CH, xpad_ref, CONV_TAIL),
                                        (w["w_u"], D_POOL, upad_ref, POOL_TAIL)):
        for col in range(0, width, PROJ_TN):
            t = tile(w_ref, col)
            for k in range(PROJ_TN // LANES):
                dst_ref[col // LANES + k, row0:row0 + STEP, :] = t[:, k * LANES:(k + 1) * LANES]
    for w_ref, width, dst0 in ((w["w_z"], D_INNER, G_Z), (w["w_zp"], D_POOL, G_ZP)):
        for col in range(0, width, PROJ_TN):
            gates_ref[:, dst0 + col:dst0 + col + PROJ_TN] = tile(w_ref, col).astype(gates_ref.dtype)


def _conv_silu(convw_ref, convb_ref, xpad_ref, s, r0):
    cs = slice(s * LANES, (s + 1) * LANES)
    conv = convb_ref[:, cs]
    for k in range(CONV_WIDTH):
        off = r0 + CONV_TAIL - (CONV_WIDTH - 1) + k
        conv = conv + xpad_ref[s, off:off + CHUNK, :] * convw_ref[k:k + 1, cs]
    return _silu(conv)


def _pool_sums(upad_ref, upad_next_ref, tmp4_ref, tmp8_ref):
    t, h = STEP, POOL_TAIL

    def win(ref, s, start, n, w):
        acc = ref[s, start:start + n, :]
        for k in range(1, w):
            acc = acc + ref[s, start - k:start - k + n, :]
        return acc

    sums, centers = [], []
    for s in range(POOL_SLABS):
        w = POOL_WINDOWS[s // POOL_SLABS_PER_GROUP]
        centers.append(upad_ref[s, h:h + t, :])
        if w <= 4:
            sums.append(win(upad_ref, s, h, t, w))
        elif w == 8:
            i4 = s - 2 * POOL_SLABS_PER_GROUP
            tmp4_ref[i4, 8:16 + t, :] = win(upad_ref, s, h - 8, t + 8, 4)
            sums.append(tmp4_ref[i4, 16:16 + t, :] + tmp4_ref[i4, 12:12 + t, :])
        else:
            i4 = s - 2 * POOL_SLABS_PER_GROUP
            i8 = s - 3 * POOL_SLABS_PER_GROUP
            tmp4_ref[i4, 0:16 + t, :] = win(upad_ref, s, h - 16, t + 16, 4)
            tmp8_ref[i8, 0:8 + t, :] = tmp4_ref[i4, 8:16 + t, :] + tmp4_ref[i4, 4:12 + t, :]
            sums.append(tmp8_ref[i8, 8:8 + t, :] + tmp8_ref[i8, 0:t, :])
        upad_next_ref[s, 0:h, :] = upad_ref[s, t:t + h, :]
    return sums, centers


def _ssd_chunk(r0, dtv, xpad_ref, gates_ref, w, state_ref, ya_ref):
    f32, bf16 = jnp.float32, jnp.bfloat16
    L = CHUNK
    rs = slice(r0, r0 + L)
    conv = functools.partial(_conv_silu, w["conv_w"], w["conv_b"], xpad_ref)
    a_neg = -jnp.exp(w["a_log"][...]) * LOG2E
    head_lane = lax.broadcasted_iota(jnp.int32, (1, LANES), 1) < N_HEADS
    acs = jnp.where(head_lane, dtv * a_neg, 0.0)
    k = 1
    while k < L:
        acs = acs + _shift_rows(acs, k)
        k *= 2
    acs_t = acs.T

    rows = lax.broadcasted_iota(jnp.int32, (L, L), 0)
    cols = lax.broadcasted_iota(jnp.int32, (L, L), 1)
    causal = rows >= cols
    first_half = lax.broadcasted_iota(jnp.int32, (L, LANES), 1) < HEAD_DIM

    for g in range(N_GROUPS):
        b_g = conv(D_INNER // LANES + g, r0)
        c_g = conv(D_INNER // LANES + N_GROUPS + g, r0)
        b_bf, c_bf = b_g.astype(bf16), c_g.astype(bf16)
        cb = lax.dot_general(c_bf, b_bf, (((1,), (1,)), ((), ())), preferred_element_type=f32)
        cb = jnp.where(causal, cb, 0.0)
        h_prev = state_ref[g]
        y_off = jnp.dot(c_bf, h_prev.astype(bf16), preferred_element_type=f32)

        a_last_g, xw_g, y_g = [], [], []
        for j in range(HEADS_PER_GROUP // 2):
            ha = g * HEADS_PER_GROUP + 2 * j
            col0 = ha * HEAD_DIM
            lo = j * LANES
            col_a = jnp.broadcast_to(acs[:, ha:ha + 1], (L, L))
            col_b = jnp.broadcast_to(acs[:, ha + 1:ha + 2], (L, L))
            row_a = acs_t[ha:ha + 1, :]
            row_b = acs_t[ha + 1:ha + 2, :]
            m_a = (cb * jnp.exp2(jnp.minimum(col_a - row_a, 0.0))).astype(bf16)
            m_b = (cb * jnp.exp2(jnp.minimum(col_b - row_b, 0.0))).astype(bf16)

            a_exp = jnp.where(first_half, col_a, col_b)
            a_last = a_exp[L - 1:L, :]
            dt_exp = jnp.where(first_half, jnp.broadcast_to(dtv[:, ha:ha + 1], (L, LANES)),
                               jnp.broadcast_to(dtv[:, ha + 1:ha + 2], (L, LANES)))
            xs = conv(col0 // LANES, r0)
            xdt = xs * dt_exp
            xdt_bf = xdt.astype(bf16)
            xw_g.append((xdt * jnp.exp2(a_last - a_exp)).astype(bf16))

            zero = jnp.zeros_like(xdt_bf)
            rhs = jnp.concatenate([jnp.where(first_half, xdt_bf, zero),
                                   jnp.where(first_half, zero, xdt_bf)], axis=0)
            lhs = jnp.concatenate([m_a, m_b], axis=1)
            y = jnp.dot(lhs, rhs, preferred_element_type=f32)
            y = y + y_off[:, lo:lo + LANES] * jnp.exp2(a_exp)
            y = y + xs * w["d_skip"][:, col0:col0 + LANES]
            zz = gates_ref[rs, G_Z + col0:G_Z + col0 + LANES].astype(f32)
            y_g.append(y * _silu(zz))
            a_last_g.append(a_last)

        chunk_decay = jnp.exp2(jnp.concatenate(a_last_g, axis=1))
        upd = jnp.dot(b_g.T.astype(bf16), jnp.concatenate(xw_g, axis=1),
                      preferred_element_type=f32)
        state_ref[g] = h_prev * chunk_decay + upd

        gn = w["gnorm_g"][:, g * GROUP_WIDTH:(g + 1) * GROUP_WIDTH]
        yg = _rms_scale(jnp.concatenate(y_g, axis=1)) * gn
        ya_ref[rs, g * GROUP_WIDTH:(g + 1) * GROUP_WIDTH] = yg.astype(bf16)


def _mix(c, cur, nxt, w, ya_ref, yb_ref, scr):
    f32, bf16 = jnp.float32, jnp.bfloat16
    xpad_ref, upad_ref, gates_ref, dt_ref = cur
    tmp4_ref, tmp8_ref, state_ref = scr

    dtv = dt_ref[...] + w["dt_bias"][...]
    dtv = jnp.maximum(dtv, 0.0) + jnp.log1p(jnp.exp(-jnp.abs(dtv)))
    for q in range(STEP // CHUNK):
        r0 = q * CHUNK
        _ssd_chunk(r0, dtv[r0:r0 + CHUNK], xpad_ref, gates_ref, w, state_ref, ya_ref)
    for s in range(CONV_SLABS):
        nxt[0][s, 0:CONV_TAIL, :] = xpad_ref[s, STEP:STEP + CONV_TAIL, :]

    sums, centers = _pool_sums(upad_ref, nxt[1], tmp4_ref, tmp8_ref)
    pos = lax.broadcasted_iota(jnp.int32, (STEP, LANES), 0) + (c * STEP + 1)
    for gi, win in enumerate(POOL_WINDOWS):
        gs = slice(gi * POOL_GROUP_DIM, (gi + 1) * POOL_GROUP_DIM)
        cnt = jnp.minimum(pos, win).astype(f32)
        pooled = [(sums[s] / cnt - centers[s]).astype(bf16)
                  for s in range(gi * POOL_SLABS_PER_GROUP, (gi + 1) * POOL_SLABS_PER_GROUP)]
        mixed = jnp.dot(jnp.concatenate(pooled, axis=1), w["mix_w"][gi], preferred_element_type=f32)
        yb = (mixed + w["mix_b"][:, gs]) * w["pool_scale"][:, gs]
        zp = gates_ref[:, G_ZP + gi * POOL_GROUP_DIM:G_ZP + (gi + 1) * POOL_GROUP_DIM].astype(f32)
        yb_ref[:, gs] = (yb * _silu(zp)).astype(bf16)


MIXER_WEIGHTS = ("norm_g", "w_xbc", "w_u", "w_z", "w_zp", "w_dt", "conv_w", "conv_b", "dt_bias", "a_log", "d_skip",
                 "gnorm_g", "mix_w", "mix_b", "pool_scale")


def _mixer_kernel(steps_per_seq, x_ref, xn_ref, *refs):
    nw = len(MIXER_WEIGHTS)
    w = dict(zip(MIXER_WEIGHTS, refs[:nw]))
    ya_ref, yb_ref = refs[nw:nw + 2]
    xpad0, upad0, gates0, dt0, xpad1, upad1, gates1, dt1 = refs[nw + 2:nw + 10]
    scr = refs[nw + 10:]
    state_ref = scr[2]
    sets = ((xpad0, upad0, gates0, dt0), (xpad1, upad1, gates1, dt1))
    i = pl.program_id(0)
    c = lax.rem(i, steps_per_seq)

    @pl.when(i == 0)
    def _():
        _project(x_ref[...], w, sets[0])

    @pl.when(c == 0)
    def _():
        xpad0[:, 0:CONV_TAIL, :] = jnp.zeros((CONV_SLABS, CONV_TAIL, LANES), jnp.float32)
        upad0[:, 0:POOL_TAIL, :] = jnp.zeros((POOL_SLABS, POOL_TAIL, LANES), jnp.float32)
        state_ref[...] = jnp.zeros(state_ref.shape, jnp.float32)

    for parity in range(2):
        @pl.when(lax.rem(i, 2) == parity)
        def _():
            cur, nxt = sets[parity], sets[1 - parity]
            _project(xn_ref[...], w, nxt)
            _mix(c, cur, nxt, w, ya_ref, yb_ref, scr)


def _const_spec(a):
    return pl.BlockSpec(a.shape, lambda i: (0,) * a.ndim, pipeline_mode=pl.Buffered(1))


def _mixer_call(x2d, weights, steps_per_seq):
    t = x2d.shape[0]
    n = t // STEP
    assert steps_per_seq % 2 == 0
    buf_set = [
        pltpu.VMEM((CONV_SLABS, CONV_TAIL + STEP, LANES), jnp.float32),
        pltpu.VMEM((POOL_SLABS, POOL_TAIL + STEP, LANES), jnp.float32),
        pltpu.VMEM((STEP, N_GATES), jnp.bfloat16),
        pltpu.VMEM((STEP, LANES), jnp.float32),
    ]
    return pl.pallas_call(
        functools.partial(_mixer_kernel, steps_per_seq),
        grid=(n,),
        in_specs=[
            pl.BlockSpec((STEP, D_MODEL), lambda i: (0, 0)),
            pl.BlockSpec((STEP, D_MODEL), lambda i: (jnp.minimum(i + 1, n - 1), 0)),
        ] + [_const_spec(a) for a in weights],
        out_specs=[
            pl.BlockSpec((STEP, D_INNER), lambda i: (i, 0)),
            pl.BlockSpec((STEP, D_POOL), lambda i: (i, 0)),
        ],
        out_shape=[
            jax.ShapeDtypeStruct((t, D_INNER), jnp.bfloat16),
            jax.ShapeDtypeStruct((t, D_POOL), jnp.bfloat16),
        ],
        scratch_shapes=buf_set + buf_set + [
            pltpu.VMEM((2 * POOL_SLABS_PER_GROUP, 16 + STEP, LANES), jnp.float32),
            pltpu.VMEM((POOL_SLABS_PER_GROUP, 8 + STEP, LANES), jnp.float32),
            pltpu.VMEM((N_GROUPS, D_STATE, GROUP_WIDTH), jnp.float32),
        ],
        compiler_params=pltpu.CompilerParams(
            dimension_semantics=("arbitrary",),
            vmem_limit_bytes=VMEM_LIMIT_BYTES),
        name="mixer",
    )(x2d, x2d, *weights)


TAIL_WEIGHTS = ("norm_g", "w_ga", "w_gb", "w_a", "w_b", "w_out", "ple_g", "w_gate", "w_up",
                "final_g")


def _tail_kernel(x_ref, p_ref, ya_ref, yb_ref, *refs):
    nw = len(TAIL_WEIGHTS)
    w = dict(zip(TAIL_WEIGHTS, refs[:nw]))
    out_ref, h_ref, merged_ref, hg_ref = refs[nw:]
    f32, bf16 = jnp.float32, jnp.bfloat16
    rows = x_ref.shape[0]
    tiles = [slice(n * TAIL_TN, (n + 1) * TAIL_TN) for n in range(D_MODEL // TAIL_TN)]

    h_ref[...] = (_rms_scale(x_ref[...]) * w["norm_g"][...]).astype(bf16)
    for ts in tiles:
        g_a = jnp.dot(h_ref[...], w["w_ga"][:, ts], preferred_element_type=f32)
        g_b = jnp.dot(h_ref[...], w["w_gb"][:, ts], preferred_element_type=f32)
        proj_a = jnp.dot(ya_ref[...], w["w_a"][:, ts], preferred_element_type=f32)
        proj_b = jnp.dot(yb_ref[...], w["w_b"][:, ts], preferred_element_type=f32)
        merged_ref[:, ts] = (_sigmoid(g_a) * proj_a + _sigmoid(g_b) * proj_b).astype(bf16)
    ssq = jnp.zeros((rows, 1), f32)
    for ts in tiles:
        x1 = x_ref[:, ts] + jnp.dot(merged_ref[...], w["w_out"][:, ts], preferred_element_type=f32)
        out_ref[:, ts] = x1
        ssq = ssq + jnp.sum(x1 * x1, axis=-1, keepdims=True)

    inv = lax.rsqrt(ssq * (1.0 / D_MODEL) + EPS)
    for ts in tiles:
        hg_ref[:, ts] = (out_ref[:, ts] * inv * w["ple_g"][:, ts]).astype(bf16)
    p_bf = p_ref[...].astype(bf16)
    ssq = jnp.zeros((rows, 1), f32)
    for ts in tiles:
        gate = _sigmoid(jnp.dot(hg_ref[...], w["w_gate"][:, ts], preferred_element_type=f32))
        up = jnp.dot(p_bf, w["w_up"][:, ts], preferred_element_type=f32)
        x2 = out_ref[:, ts] + gate * up
        out_ref[:, ts] = x2
        ssq = ssq + jnp.sum(x2 * x2, axis=-1, keepdims=True)
    inv = lax.rsqrt(ssq * (1.0 / D_MODEL) + EPS)
    for ts in tiles:
        out_ref[:, ts] = out_ref[:, ts] * inv * w["final_g"][:, ts]


def _tail_call(x2d, p2d, ya, yb, weights):
    t = x2d.shape[0]
    tb = TAIL_ROWS if t % TAIL_ROWS == 0 else STEP
    row_spec = lambda width: pl.BlockSpec((tb, width), lambda i: (i, 0))
    return pl.pallas_call(
        _tail_kernel,
        grid=(t // tb,),
        in_specs=[row_spec(D_MODEL), row_spec(PLE_DIM), row_spec(D_INNER), row_spec(D_POOL)]
        + [_const_spec(a) for a in weights],
        out_specs=row_spec(D_MODEL),
        out_shape=jax.ShapeDtypeStruct((t, D_MODEL), jnp.float32),
        scratch_shapes=[
            pltpu.VMEM((tb, D_MODEL), jnp.bfloat16),
            pltpu.VMEM((tb, D_MODEL), jnp.bfloat16),
            pltpu.VMEM((tb, D_MODEL), jnp.bfloat16),
        ],
        compiler_params=pltpu.CompilerParams(
            dimension_semantics=("arbitrary",),
            vmem_limit_bytes=VMEM_LIMIT_BYTES),
        name="tail",
    )(x2d, p2d, ya, yb, *weights)


def _row(v, width=None):
    v = v.reshape(1, -1).astype(jnp.float32)
    if width is not None and v.shape[1] < width:
        v = jnp.pad(v, ((0, 0), (0, width - v.shape[1])))
    return v


def kernel(x, p, norm_g, w_in, conv_w, conv_b, dt_bias, a_log, d_skip, gnorm_g, pool_mix_w,
           pool_mix_b, pool_scale, w_branch_a, w_branch_b, w_out, ple_norm_g, w_ple_gate,
           w_ple_up, final_g):
    depth = w_in.shape[0]
    bsz, seqlen, _ = x.shape
    assert seqlen % (2 * STEP) == 0
    assert depth == 1
    bf16 = jnp.bfloat16
    o_z, o_xbc, o_dt = 0, D_INNER, D_INNER + D_CONV_CH
    o_u = o_dt + N_HEADS
    o_zp, o_ga, o_gb = o_u + D_POOL, o_u + 2 * D_POOL, o_u + 3 * D_POOL
    t = bsz * seqlen
    i = 0
    wi = w_in.reshape(D_MODEL, -1)
    w_dt = jnp.pad(wi[:, o_dt:o_dt + N_HEADS], ((0, 0), (0, LANES - N_HEADS))).astype(bf16)
    mixer_weights = (
        _row(norm_g[i]), wi[:, o_xbc:o_xbc + D_CONV_CH].astype(bf16),
        wi[:, o_u:o_u + D_POOL].astype(bf16), wi[:, o_z:o_z + D_INNER].astype(bf16),
        wi[:, o_zp:o_zp + D_POOL].astype(bf16), w_dt,
        conv_w[i].astype(jnp.float32), _row(conv_b[i]), _row(dt_bias[i], LANES),
        _row(a_log[i], LANES), _row(jnp.repeat(d_skip[i], HEAD_DIM)), _row(gnorm_g[i]),
        pool_mix_w[i].astype(bf16), _row(pool_mix_b[i]), _row(pool_scale[i]),
    )
    tail_weights = (
        _row(norm_g[i]), wi[:, o_ga:o_ga + D_MODEL].astype(bf16),
        wi[:, o_gb:o_gb + D_MODEL].astype(bf16),
        w_branch_a[i].astype(bf16), w_branch_b[i].astype(bf16), w_out[i].astype(bf16),
        _row(ple_norm_g[i]), w_ple_gate[i].astype(bf16), w_ple_up[i].astype(bf16),
        _row(final_g),
    )
    x2d = x.reshape(t, D_MODEL)
    ya, yb = _mixer_call(x2d, mixer_weights, seqlen // STEP)
    out = _tail_call(x2d, p.reshape(depth * t, PLE_DIM), ya, yb, tail_weights)
    return out.reshape(bsz, seqlen, D_MODEL)
```

```python
import functools
import math

import jax
import jax.numpy as jnp
from jax import lax
from jax.experimental import pallas as pl
from jax.experimental.pallas import tpu as pltpu

D_MODEL = 1024
PLE_DIM = 256
D_INNER = 2048
HEAD_DIM = 64
N_HEADS = 32
N_GROUPS = 4
HEADS_PER_GROUP = 8
D_STATE = 128
CONV_WIDTH = 4
D_CONV_CH = D_INNER + 2 * N_GROUPS * D_STATE
D_POOL = 1024
POOL_WINDOWS = (2, 4, 8, 16)
POOL_GROUP_DIM = 256
GROUP_WIDTH = D_INNER // N_GROUPS
EPS = 1e-6
LOG2E = math.log2(math.e)

LANES = 128
SUBLANES = 8
VMEM_BYTES_V7X = 64 * 1024 * 1024
VMEM_LIMIT_BYTES = VMEM_BYTES_V7X - 3 * 1024 * 1024

CHUNK = 128
STEP = 2 * CHUNK
CONV_TAIL = SUBLANES
POOL_TAIL = 3 * SUBLANES
CONV_SLABS = D_CONV_CH // LANES
POOL_SLABS = D_POOL // LANES
POOL_SLABS_PER_GROUP = POOL_GROUP_DIM // LANES

N_GATES = D_INNER + D_POOL
G_Z, G_ZP = 0, D_INNER
TAIL_ROWS = 1024
PROJ_TN = 256
TAIL_TN = 256


def _sigmoid(v):
    return 0.5 + 0.5 * jnp.tanh(0.5 * v)


def _silu(v):
    h = 0.5 * v
    return h + h * jnp.tanh(h)


def _rms_scale(v):
    return v * lax.rsqrt(jnp.mean(v * v, axis=-1, keepdims=True) + EPS)


def _shift_rows(v, k):
    rows = lax.broadcasted_iota(jnp.int32, v.shape, 0)
    return jnp.where(rows >= k, pltpu.roll(v, k, axis=0), 0.0)


def _project(x, w, bufs):
    xpad_ref, upad_ref, gates_ref, dt_ref = bufs
    f32 = jnp.float32
    h = (_rms_scale(x) * w["norm_g"][...]).astype(jnp.bfloat16)

    def tile(w_ref, col):
        return jnp.dot(h, w_ref[:, col:col + PROJ_TN], preferred_element_type=f32)

    dt_ref[...] = jnp.dot(h, w["w_dt"][...], preferred_element_type=f32)
    for w_ref, width, dst_ref, row0 in ((w["w_xbc"], D_CONV_CH, xpad_ref, CONV_TAIL),
                                        (w["w_u"], D_POOL, upad_ref, POOL_TAIL)):
        for col in range(0, width, PROJ_TN):
            t = tile(w_ref, col)
            for k in range(PROJ_TN // LANES):
                dst_ref[col // LANES + k, row0:row0 + STEP, :] = t[:, k * LANES:(k + 1) * LANES]
    for w_ref, width, dst0 in ((w["w_z"], D_INNER, G_Z), (w["w_zp"], D_POOL, G_ZP)):
        for col in range(0, width, PROJ_TN):
            gates_ref[:, dst0 + col:dst0 + col + PROJ_TN] = tile(w_ref, col).astype(gates_ref.dtype)


def _conv_silu(convw_ref, convb_ref, xpad_ref, s, r0):
    cs = slice(s * LANES, (s + 1) * LANES)
    conv = convb_ref[:, cs]
    for k in range(CONV_WIDTH):
        off = r0 + CONV_TAIL - (CONV_WIDTH - 1) + k
        conv = conv + xpad_ref[s, off:off + CHUNK, :] * convw_ref[k:k + 1, cs]
    return _silu(conv)


def _pool_sums(upad_ref, upad_next_ref, tmp4_ref, tmp8_ref):
    t, h = STEP, POOL_TAIL

    def win(ref, s, start, n, w):
        acc = ref[s, start:start + n, :]
        for k in range(1, w):
            acc = acc + ref[s, start - k:start - k + n, :]
        return acc

    sums, centers = [], []
    for s in range(POOL_SLABS):
        w = POOL_WINDOWS[s // POOL_SLABS_PER_GROUP]
        centers.append(upad_ref[s, h:h + t, :])
        if w <= 4:
            sums.append(win(upad_ref, s, h, t, w))
        elif w == 8:
            i4 = s - 2 * POOL_SLABS_PER_GROUP
            tmp4_ref[i4, 8:16 + t, :] = win(upad_ref, s, h - 8, t + 8, 4)
            sums.append(tmp4_ref[i4, 16:16 + t, :] + tmp4_ref[i4, 12:12 + t, :])
        else:
            i4 = s - 2 * POOL_SLABS_PER_GROUP
            i8 = s - 3 * POOL_SLABS_PER_GROUP
            tmp4_ref[i4, 0:16 + t, :] = win(upad_ref, s, h - 16, t + 16, 4)
            tmp8_ref[i8, 0:8 + t, :] = tmp4_ref[i4, 8:16 + t, :] + tmp4_ref[i4, 4:12 + t, :]
            sums.append(tmp8_ref[i8, 8:8 + t, :] + tmp8_ref[i8, 0:t, :])
        upad_next_ref[s, 0:h, :] = upad_ref[s, t:t + h, :]
    return sums, centers


def _ssd_chunk(r0, dtv, xpad_ref, gates_ref, w, state_ref, ya_ref):
    f32, bf16 = jnp.float32, jnp.bfloat16
    L = CHUNK
    rs = slice(r0, r0 + L)
    conv = functools.partial(_conv_silu, w["conv_w"], w["conv_b"], xpad_ref)
    a_neg = -jnp.exp(w["a_log"][...]) * LOG2E
    head_lane = lax.broadcasted_iota(jnp.int32, (1, LANES), 1) < N_HEADS
    acs = jnp.where(head_lane, dtv * a_neg, 0.0)
    k = 1
    while k < L:
        acs = acs + _shift_rows(acs, k)
        k *= 2
    acs_t = acs.T

    rows = lax.broadcasted_iota(jnp.int32, (L, L), 0)
    cols = lax.broadcasted_iota(jnp.int32, (L, L), 1)
    causal = rows >= cols
    first_half = lax.broadcasted_iota(jnp.int32, (L, LANES), 1) < HEAD_DIM

    for g in range(N_GROUPS):
        b_g = conv(D_INNER // LANES + g, r0)
        c_g = conv(D_INNER // LANES + N_GROUPS + g, r0)
        b_bf, c_bf = b_g.astype(bf16), c_g.astype(bf16)
        cb = lax.dot_general(c_bf, b_bf, (((1,), (1,)), ((), ())), preferred_element_type=f32)
        cb = jnp.where(causal, cb, 0.0)
        h_prev = state_ref[g]
        y_off = jnp.dot(c_bf, h_prev.astype(bf16), preferred_element_type=f32)

        a_last_g, xw_g, y_g = [], [], []
        for j in range(HEADS_PER_GROUP // 2):
            ha = g * HEADS_PER_GROUP + 2 * j
            col0 = ha * HEAD_DIM
            lo = j * LANES
            col_a = jnp.broadcast_to(acs[:, ha:ha + 1], (L, L))
            col_b = jnp.broadcast_to(acs[:, ha + 1:ha + 2], (L, L))
            row_a = acs_t[ha:ha + 1, :]
            row_b = acs_t[ha + 1:ha + 2, :]
            m_a = (cb * jnp.exp2(jnp.minimum(col_a - row_a, 0.0))).astype(bf16)
            m_b = (cb * jnp.exp2(jnp.minimum(col_b - row_b, 0.0))).astype(bf16)

            a_exp = jnp.where(first_half, col_a, col_b)
            a_last = a_exp[L - 1:L, :]
            dt_exp = jnp.where(first_half, jnp.broadcast_to(dtv[:, ha:ha + 1], (L, LANES)),
                               jnp.broadcast_to(dtv[:, ha + 1:ha + 2], (L, LANES)))
            xs = conv(col0 // LANES, r0)
            xdt = xs * dt_exp
            xdt_bf = xdt.astype(bf16)
            xw_g.append((xdt * jnp.exp2(a_last - a_exp)).astype(bf16))

            zero = jnp.zeros_like(xdt_bf)
            rhs = jnp.concatenate([jnp.where(first_half, xdt_bf, zero),
                                   jnp.where(first_half, zero, xdt_bf)], axis=0)
            lhs = jnp.concatenate([m_a, m_b], axis=1)
            y = jnp.dot(lhs, rhs, preferred_element_type=f32)
            y = y + y_off[:, lo:lo + LANES] * jnp.exp2(a_exp)
            y = y + xs * w["d_skip"][:, col0:col0 + LANES]
            zz = gates_ref[rs, G_Z + col0:G_Z + col0 + LANES].astype(f32)
            y_g.append(y * _silu(zz))
            a_last_g.append(a_last)

        chunk_decay = jnp.exp2(jnp.concatenate(a_last_g, axis=1))
        upd = jnp.dot(b_g.T.astype(bf16), jnp.concatenate(xw_g, axis=1),
                      preferred_element_type=f32)
        state_ref[g] = h_prev * chunk_decay + upd

        gn = w["gnorm_g"][:, g * GROUP_WIDTH:(g + 1) * GROUP_WIDTH]
        yg = _rms_scale(jnp.concatenate(y_g, axis=1)) * gn
        ya_ref[rs, g * GROUP_WIDTH:(g + 1) * GROUP_WIDTH] = yg.astype(bf16)


def _mix(c, cur, nxt, w, ya_ref, yb_ref, scr):
    f32, bf16 = jnp.float32, jnp.bfloat16
    xpad_ref, upad_ref, gates_ref, dt_ref = cur
    tmp4_ref, tmp8_ref, state_ref = scr

    dtv = dt_ref[...] + w["dt_bias"][...]
    dtv = jnp.maximum(dtv, 0.0) + jnp.log1p(jnp.exp(-jnp.abs(dtv)))
    for q in range(STEP // CHUNK):
        r0 = q * CHUNK
        _ssd_chunk(r0, dtv[r0:r0 + CHUNK], xpad_ref, gates_ref, w, state_ref, ya_ref)
    for s in range(CONV_SLABS):
        nxt[0][s, 0:CONV_TAIL, :] = xpad_ref[s, STEP:STEP + CONV_TAIL, :]

    sums, centers = _pool_sums(upad_ref, nxt[1], tmp4_ref, tmp8_ref)
    pos = lax.broadcasted_iota(jnp.int32, (STEP, LANES), 0) + (c * STEP + 1)
    for gi, win in enumerate(POOL_WINDOWS):
        gs = slice(gi * POOL_GROUP_DIM, (gi + 1) * POOL_GROUP_DIM)
        cnt = jnp.minimum(pos, win).astype(f32)
        pooled = [(sums[s] / cnt - centers[s]).astype(bf16)
                  for s in range(gi * POOL_SLABS_PER_GROUP, (gi + 1) * POOL_SLABS_PER_GROUP)]
        mixed = jnp.dot(jnp.concatenate(pooled, axis=1), w["mix_w"][gi], preferred_element_type=f32)
        yb = (mixed + w["mix_b"][:, gs]) * w["pool_scale"][:, gs]
        zp = gates_ref[:, G_ZP + gi * POOL_GROUP_DIM:G_ZP + (gi + 1) * POOL_GROUP_DIM].astype(f32)
        yb_ref[:, gs] = (yb * _silu(zp)).astype(bf16)


MIXER_WEIGHTS = ("norm_g", "w_xbc", "w_u", "w_z", "w_zp", "w_dt", "conv_w", "conv_b", "dt_bias", "a_log", "d_skip",
                 "gnorm_g", "mix_w", "mix_b", "pool_scale")


def _mixer_kernel(steps_per_seq, x_ref, xn_ref, *refs):
    nw = len(MIXER_WEIGHTS)
    w = dict(zip(MIXER_WEIGHTS, refs[:nw]))
    ya_ref, yb_ref = refs[nw:nw + 2]
    xpad0, upad0, gates0, dt0, xpad1, upad1, gates1, dt1 = refs[nw + 2:nw + 10]
    scr = refs[nw + 10:]
    state_ref = scr[2]
    sets = ((xpad0, upad0, gates0, dt0), (xpad1, upad1, gates1, dt1))
    i = pl.program_id(0)
    c = lax.rem(i, steps_per_seq)

    @pl.when(i == 0)
    def _():
        _project(x_ref[...], w, sets[0])

    @pl.when(c == 0)
    def _():
        xpad0[:, 0:CONV_TAIL, :] = jnp.zeros((CONV_SLABS, CONV_TAIL, LANES), jnp.float32)
        upad0[:, 0:POOL_TAIL, :] = jnp.zeros((POOL_SLABS, POOL_TAIL, LANES), jnp.float32)
        state_ref[...] = jnp.zeros(state_ref.shape, jnp.float32)

    for parity in range(2):
        @pl.when(lax.rem(i, 2) == parity)
        def _():
            cur, nxt = sets[parity], sets[1 - parity]
            _project(xn_ref[...], w, nxt)
            _mix(c, cur, nxt, w, ya_ref, yb_ref, scr)


def _const_spec(a):
    return pl.BlockSpec(a.shape, lambda i: (0,) * a.ndim, pipeline_mode=pl.Buffered(1))


def _mixer_call(x2d, weights, steps_per_seq):
    t = x2d.shape[0]
    n = t // STEP
    assert steps_per_seq % 2 == 0
    buf_set = [
        pltpu.VMEM((CONV_SLABS, CONV_TAIL + STEP, LANES), jnp.float32),
        pltpu.VMEM((POOL_SLABS, POOL_TAIL + STEP, LANES), jnp.float32),
        pltpu.VMEM((STEP, N_GATES), jnp.float32),
        pltpu.VMEM((STEP, LANES), jnp.float32),
    ]
    return pl.pallas_call(
        functools.partial(_mixer_kernel, steps_per_seq),
        grid=(n,),
        in_specs=[
            pl.BlockSpec((STEP, D_MODEL), lambda i: (0, 0)),
            pl.BlockSpec((STEP, D_MODEL), lambda i: (jnp.minimum(i + 1, n - 1), 0)),
        ] + [_const_spec(a) for a in weights],
        out_specs=[
            pl.BlockSpec((STEP, D_INNER), lambda i: (i, 0)),
            pl.BlockSpec((STEP, D_POOL), lambda i: (i, 0)),
        ],
        out_shape=[
            jax.ShapeDtypeStruct((t, D_INNER), jnp.bfloat16),
            jax.ShapeDtypeStruct((t, D_POOL), jnp.bfloat16),
        ],
        scratch_shapes=buf_set + buf_set + [
            pltpu.VMEM((2 * POOL_SLABS_PER_GROUP, 16 + STEP, LANES), jnp.float32),
            pltpu.VMEM((POOL_SLABS_PER_GROUP, 8 + STEP, LANES), jnp.float32),
            pltpu.VMEM((N_GROUPS, D_STATE, GROUP_WIDTH), jnp.float32),
        ],
        compiler_params=pltpu.CompilerParams(
            dimension_semantics=("arbitrary",),
            vmem_limit_bytes=VMEM_LIMIT_BYTES),
        name="mixer",
    )(x2d, x2d, *weights)


TAIL_WEIGHTS = ("norm_g", "w_ga", "w_gb", "w_a", "w_b", "w_out", "ple_g", "w_gate", "w_up",
                "final_g")


def _tail_kernel(x_ref, p_ref, ya_ref, yb_ref, *refs):
    nw = len(TAIL_WEIGHTS)
    w = dict(zip(TAIL_WEIGHTS, refs[:nw]))
    out_ref, h_ref, merged_ref, hg_ref = refs[nw:]
    f32, bf16 = jnp.float32, jnp.bfloat16
    rows = x_ref.shape[0]
    tiles = [slice(n * TAIL_TN, (n + 1) * TAIL_TN) for n in range(D_MODEL // TAIL_TN)]

    h_ref[...] = (_rms_scale(x_ref[...]) * w["norm_g"][...]).astype(bf16)
    for ts in tiles:
        g_a = jnp.dot(h_ref[...], w["w_ga"][:, ts], preferred_element_type=f32)
        g_b = jnp.dot(h_ref[...], w["w_gb"][:, ts], preferred_element_type=f32)
        proj_a = jnp.dot(ya_ref[...], w["w_a"][:, ts], preferred_element_type=f32)
        proj_b = jnp.dot(yb_ref[...], w["w_b"][:, ts], preferred_element_type=f32)
        merged_ref[:, ts] = (_sigmoid(g_a) * proj_a + _sigmoid(g_b) * proj_b).astype(bf16)
    ssq = jnp.zeros((rows, 1), f32)
    for ts in tiles:
        x1 = x_ref[:, ts] + jnp.dot(merged_ref[...], w["w_out"][:, ts], preferred_element_type=f32)
        out_ref[:, ts] = x1
        ssq = ssq + jnp.sum(x1 * x1, axis=-1, keepdims=True)

    inv = lax.rsqrt(ssq * (1.0 / D_MODEL) + EPS)
    for ts in tiles:
        hg_ref[:, ts] = (out_ref[:, ts] * inv * w["ple_g"][:, ts]).astype(bf16)
    p_bf = p_ref[...].astype(bf16)
    ssq = jnp.zeros((rows, 1), f32)
    for ts in tiles:
        gate = _sigmoid(jnp.dot(hg_ref[...], w["w_gate"][:, ts], preferred_element_type=f32))
        up = jnp.dot(p_bf, w["w_up"][:, ts], preferred_element_type=f32)
        x2 = out_ref[:, ts] + gate * up
        out_ref[:, ts] = x2
        ssq = ssq + jnp.sum(x2 * x2, axis=-1, keepdims=True)
    inv = lax.rsqrt(ssq * (1.0 / D_MODEL) + EPS)
    for ts in tiles:
        out_ref[:, ts] = out_ref[:, ts] * inv * w["final_g"][:, ts]


def _tail_call(x2d, p2d, ya, yb, weights):
    t = x2d.shape[0]
    tb = TAIL_ROWS if t % TAIL_ROWS == 0 else STEP
    row_spec = lambda width: pl.BlockSpec((tb, width), lambda i: (i, 0))
    return pl.pallas_call(
        _tail_kernel,
        grid=(t // tb,),
        in_specs=[row_spec(D_MODEL), row_spec(PLE_DIM), row_spec(D_INNER), row_spec(D_POOL)]
        + [_const_spec(a) for a in weights],
        out_specs=row_spec(D_MODEL),
        out_shape=jax.ShapeDtypeStruct((t, D_MODEL), jnp.float32),
        scratch_shapes=[
            pltpu.VMEM((tb, D_MODEL), jnp.bfloat16),
            pltpu.VMEM((tb, D_MODEL), jnp.bfloat16),
            pltpu.VMEM((tb, D_MODEL), jnp.bfloat16),
        ],
        compiler_params=pltpu.CompilerParams(
            dimension_semantics=("arbitrary",),
            vmem_limit_bytes=VMEM_LIMIT_BYTES),
        name="tail",
    )(x2d, p2d, ya, yb, *weights)


def _row(v, width=None):
    v = v.reshape(1, -1).astype(jnp.float32)
    if width is not None and v.shape[1] < width:
        v = jnp.pad(v, ((0, 0), (0, width - v.shape[1])))
    return v


def kernel(x, p, norm_g, w_in, conv_w, conv_b, dt_bias, a_log, d_skip, gnorm_g, pool_mix_w,
           pool_mix_b, pool_scale, w_branch_a, w_branch_b, w_out, ple_norm_g, w_ple_gate,
           w_ple_up, final_g):
    depth = w_in.shape[0]
    bsz, seqlen, _ = x.shape
    assert seqlen % (2 * STEP) == 0
    assert depth == 1
    bf16 = jnp.bfloat16
    o_z, o_xbc, o_dt = 0, D_INNER, D_INNER + D_CONV_CH
    o_u = o_dt + N_HEADS
    o_zp, o_ga, o_gb = o_u + D_POOL, o_u + 2 * D_POOL, o_u + 3 * D_POOL
    t = bsz * seqlen
    i = 0
    wi = w_in.reshape(D_MODEL, -1)
    w_dt = jnp.pad(wi[:, o_dt:o_dt + N_HEADS], ((0, 0), (0, LANES - N_HEADS))).astype(bf16)
    mixer_weights = (
        _row(norm_g[i]), wi[:, o_xbc:o_xbc + D_CONV_CH].astype(bf16),
        wi[:, o_u:o_u + D_POOL].astype(bf16), wi[:, o_z:o_z + D_INNER].astype(bf16),
        wi[:, o_zp:o_zp + D_POOL].astype(bf16), w_dt,
        conv_w[i].astype(jnp.float32), _row(conv_b[i]), _row(dt_bias[i], LANES),
        _row(a_log[i], LANES), _row(jnp.repeat(d_skip[i], HEAD_DIM)), _row(gnorm_g[i]),
        pool_mix_w[i].astype(bf16), _row(pool_mix_b[i]), _row(pool_scale[i]),
    )
    tail_weights = (
        _row(norm_g[i]), wi[:, o_ga:o_ga + D_MODEL].astype(bf16),
        wi[:, o_gb:o_gb + D_MODEL].astype(bf16),
        w_branch_a[i].astype(bf16), w_branch_b[i].astype(bf16), w_out[i].astype(bf16),
        _row(ple_norm_g[i]), w_ple_gate[i].astype(bf16), w_ple_up[i].astype(bf16),
        _row(final_g),
    )
    x2d = x.reshape(t, D_MODEL)
    ya, yb = _mixer_call(x2d, mixer_weights, seqlen // STEP)
    out = _tail_call(x2d, p.reshape(depth * t, PLE_DIM), ya, yb, tail_weights)
    return out.reshape(bsz, seqlen, D_MODEL)
```

```python
import functools
import math

import jax
import jax.numpy as jnp
from jax import lax
from jax.experimental import pallas as pl
from jax.experimental.pallas import tpu as pltpu

D_MODEL = 1024
PLE_DIM = 256
D_INNER = 2048
HEAD_DIM = 64
N_HEADS = 32
N_GROUPS = 4
HEADS_PER_GROUP = 8
D_STATE = 128
CONV_WIDTH = 4
D_CONV_CH = D_INNER + 2 * N_GROUPS * D_STATE
D_POOL = 1024
POOL_WINDOWS = (2, 4, 8, 16)
POOL_GROUP_DIM = 256
GROUP_WIDTH = D_INNER // N_GROUPS
EPS = 1e-6
LOG2E = math.log2(math.e)

LANES = 128
SUBLANES = 8
VMEM_BYTES_V7X = 64 * 1024 * 1024
VMEM_LIMIT_BYTES = VMEM_BYTES_V7X - 3 * 1024 * 1024

CHUNK = 128
STEP = 2 * CHUNK
CONV_TAIL = SUBLANES
POOL_TAIL = 3 * SUBLANES
CONV_SLABS = D_CONV_CH // LANES
POOL_SLABS = D_POOL // LANES
POOL_SLABS_PER_GROUP = POOL_GROUP_DIM // LANES

N_GATES = D_INNER
G_Z = 0
TAIL_ROWS = 1024
PROJ_TN = 256
TAIL_TN = 256


def _sigmoid(v):
    return 0.5 + 0.5 * jnp.tanh(0.5 * v)


def _silu(v):
    h = 0.5 * v
    return h + h * jnp.tanh(h)


def _rms_scale(v):
    return v * lax.rsqrt(jnp.mean(v * v, axis=-1, keepdims=True) + EPS)


def _shift_rows(v, k):
    rows = lax.broadcasted_iota(jnp.int32, v.shape, 0)
    return jnp.where(rows >= k, pltpu.roll(v, k, axis=0), 0.0)


def _project(x, w, bufs):
    xpad_ref, upad_ref, gates_ref, dt_ref = bufs
    f32 = jnp.float32
    h = (_rms_scale(x) * w["norm_g"][...]).astype(jnp.bfloat16)

    def tile(w_ref, col):
        return jnp.dot(h, w_ref[:, col:col + PROJ_TN], preferred_element_type=f32)

    dt_ref[...] = jnp.dot(h, w["w_dt"][...], preferred_element_type=f32)
    for w_ref, width, dst_ref, row0 in ((w["w_xbc"], D_CONV_CH, xpad_ref, CONV_TAIL),
                                        (w["w_u"], D_POOL, upad_ref, POOL_TAIL)):
        for col in range(0, width, PROJ_TN):
            t = tile(w_ref, col)
            for k in range(PROJ_TN // LANES):
                dst_ref[col // LANES + k, row0:row0 + STEP, :] = t[:, k * LANES:(k + 1) * LANES]
    for w_ref, width, dst0 in ((w["w_z"], D_INNER, G_Z),):
        for col in range(0, width, PROJ_TN):
            gates_ref[:, dst0 + col:dst0 + col + PROJ_TN] = tile(w_ref, col).astype(gates_ref.dtype)


def _conv_silu(convw_ref, convb_ref, xpad_ref, s, r0):
    cs = slice(s * LANES, (s + 1) * LANES)
    conv = convb_ref[:, cs]
    for k in range(CONV_WIDTH):
        off = r0 + CONV_TAIL - (CONV_WIDTH - 1) + k
        conv = conv + xpad_ref[s, off:off + CHUNK, :] * convw_ref[k:k + 1, cs]
    return _silu(conv)


def _pool_sums(upad_ref, upad_next_ref, tmp4_ref, tmp8_ref):
    t, h = STEP, POOL_TAIL

    def win(ref, s, start, n, w):
        acc = ref[s, start:start + n, :]
        for k in range(1, w):
            acc = acc + ref[s, start - k:start - k + n, :]
        return acc

    sums, centers = [], []
    for s in range(POOL_SLABS):
        w = POOL_WINDOWS[s // POOL_SLABS_PER_GROUP]
        centers.append(upad_ref[s, h:h + t, :])
        if w <= 4:
            sums.append(win(upad_ref, s, h, t, w))
        elif w == 8:
            i4 = s - 2 * POOL_SLABS_PER_GROUP
            tmp4_ref[i4, 8:16 + t, :] = win(upad_ref, s, h - 8, t + 8, 4)
            sums.append(tmp4_ref[i4, 16:16 + t, :] + tmp4_ref[i4, 12:12 + t, :])
        else:
            i4 = s - 2 * POOL_SLABS_PER_GROUP
            i8 = s - 3 * POOL_SLABS_PER_GROUP
            tmp4_ref[i4, 0:16 + t, :] = win(upad_ref, s, h - 16, t + 16, 4)
            tmp8_ref[i8, 0:8 + t, :] = tmp4_ref[i4, 8:16 + t, :] + tmp4_ref[i4, 4:12 + t, :]
            sums.append(tmp8_ref[i8, 8:8 + t, :] + tmp8_ref[i8, 0:t, :])
        upad_next_ref[s, 0:h, :] = upad_ref[s, t:t + h, :]
    return sums, centers


def _ssd_chunk(r0, dtv, xpad_ref, gates_ref, w, state_ref, ya_ref):
    f32, bf16 = jnp.float32, jnp.bfloat16
    L = CHUNK
    rs = slice(r0, r0 + L)
    conv = functools.partial(_conv_silu, w["conv_w"], w["conv_b"], xpad_ref)
    a_neg = -jnp.exp(w["a_log"][...]) * LOG2E
    head_lane = lax.broadcasted_iota(jnp.int32, (1, LANES), 1) < N_HEADS
    acs = jnp.where(head_lane, dtv * a_neg, 0.0)
    k = 1
    while k < L:
        acs = acs + _shift_rows(acs, k)
        k *= 2
    acs_t = acs.T

    rows = lax.broadcasted_iota(jnp.int32, (L, L), 0)
    cols = lax.broadcasted_iota(jnp.int32, (L, L), 1)
    causal = rows >= cols
    first_half = lax.broadcasted_iota(jnp.int32, (L, LANES), 1) < HEAD_DIM

    for g in range(N_GROUPS):
        b_g = conv(D_INNER // LANES + g, r0)
        c_g = conv(D_INNER // LANES + N_GROUPS + g, r0)
        b_bf, c_bf = b_g.astype(bf16), c_g.astype(bf16)
        cb = lax.dot_general(c_bf, b_bf, (((1,), (1,)), ((), ())), preferred_element_type=f32)
        cb = jnp.where(causal, cb, 0.0)
        h_prev = state_ref[g]
        y_off = jnp.dot(c_bf, h_prev.astype(bf16), preferred_element_type=f32)

        a_last_g, xw_g, y_g = [], [], []
        for j in range(HEADS_PER_GROUP // 2):
            ha = g * HEADS_PER_GROUP + 2 * j
            col0 = ha * HEAD_DIM
            lo = j * LANES
            col_a = jnp.broadcast_to(acs[:, ha:ha + 1], (L, L))
            col_b = jnp.broadcast_to(acs[:, ha + 1:ha + 2], (L, L))
            row_a = acs_t[ha:ha + 1, :]
            row_b = acs_t[ha + 1:ha + 2, :]
            m_a = (cb * jnp.exp2(jnp.minimum(col_a - row_a, 0.0))).astype(bf16)
            m_b = (cb * jnp.exp2(jnp.minimum(col_b - row_b, 0.0))).astype(bf16)

            a_exp = jnp.where(first_half, col_a, col_b)
            a_last = a_exp[L - 1:L, :]
            dt_exp = jnp.where(first_half, jnp.broadcast_to(dtv[:, ha:ha + 1], (L, LANES)),
                               jnp.broadcast_to(dtv[:, ha + 1:ha + 2], (L, LANES)))
            xs = conv(col0 // LANES, r0)
            xdt = xs * dt_exp
            xdt_bf = xdt.astype(bf16)
            xw_g.append((xdt * jnp.exp2(a_last - a_exp)).astype(bf16))

            zero = jnp.zeros_like(xdt_bf)
            rhs = jnp.concatenate([jnp.where(first_half, xdt_bf, zero),
                                   jnp.where(first_half, zero, xdt_bf)], axis=0)
            lhs = jnp.concatenate([m_a, m_b], axis=1)
            y = jnp.dot(lhs, rhs, preferred_element_type=f32)
            y = y + y_off[:, lo:lo + LANES] * jnp.exp2(a_exp)
            y = y + xs * w["d_skip"][:, col0:col0 + LANES]
            zz = gates_ref[rs, G_Z + col0:G_Z + col0 + LANES].astype(f32)
            y_g.append(y * _silu(zz))
            a_last_g.append(a_last)

        chunk_decay = jnp.exp2(jnp.concatenate(a_last_g, axis=1))
        upd = jnp.dot(b_g.T.astype(bf16), jnp.concatenate(xw_g, axis=1),
                      preferred_element_type=f32)
        state_ref[g] = h_prev * chunk_decay + upd

        gn = w["gnorm_g"][:, g * GROUP_WIDTH:(g + 1) * GROUP_WIDTH]
        yg = _rms_scale(jnp.concatenate(y_g, axis=1)) * gn
        ya_ref[rs, g * GROUP_WIDTH:(g + 1) * GROUP_WIDTH] = yg.astype(bf16)


def _mix(c, cur, nxt, w, ya_ref, yb_ref, scr):
    f32, bf16 = jnp.float32, jnp.bfloat16
    xpad_ref, upad_ref, gates_ref, dt_ref = cur
    tmp4_ref, tmp8_ref, state_ref = scr

    dtv = dt_ref[...] + w["dt_bias"][...]
    dtv = jnp.maximum(dtv, 0.0) + jnp.log1p(jnp.exp(-jnp.abs(dtv)))
    for q in range(STEP // CHUNK):
        r0 = q * CHUNK
        _ssd_chunk(r0, dtv[r0:r0 + CHUNK], xpad_ref, gates_ref, w, state_ref, ya_ref)
    for s in range(CONV_SLABS):
        nxt[0][s, 0:CONV_TAIL, :] = xpad_ref[s, STEP:STEP + CONV_TAIL, :]

    sums, centers = _pool_sums(upad_ref, nxt[1], tmp4_ref, tmp8_ref)
    pos = lax.broadcasted_iota(jnp.int32, (STEP, LANES), 0) + (c * STEP + 1)
    for gi, win in enumerate(POOL_WINDOWS):
        gs = slice(gi * POOL_GROUP_DIM, (gi + 1) * POOL_GROUP_DIM)
        cnt = jnp.minimum(pos, win).astype(f32)
        pooled = [(sums[s] / cnt - centers[s]).astype(bf16)
                  for s in range(gi * POOL_SLABS_PER_GROUP, (gi + 1) * POOL_SLABS_PER_GROUP)]
        mixed = jnp.dot(jnp.concatenate(pooled, axis=1), w["mix_w"][gi], preferred_element_type=f32)
        yb = (mixed + w["mix_b"][:, gs]) * w["pool_scale"][:, gs]
        yb_ref[:, gs] = yb.astype(bf16)


MIXER_WEIGHTS = ("norm_g", "w_xbc", "w_u", "w_z", "w_dt", "conv_w", "conv_b", "dt_bias", "a_log", "d_skip",
                 "gnorm_g", "mix_w", "mix_b", "pool_scale")


def _mixer_kernel(steps_per_seq, x_ref, xn_ref, *refs):
    nw = len(MIXER_WEIGHTS)
    w = dict(zip(MIXER_WEIGHTS, refs[:nw]))
    ya_ref, yb_ref = refs[nw:nw + 2]
    xpad0, upad0, gates0, dt0, xpad1, upad1, gates1, dt1 = refs[nw + 2:nw + 10]
    scr = refs[nw + 10:]
    state_ref = scr[2]
    sets = ((xpad0, upad0, gates0, dt0), (xpad1, upad1, gates1, dt1))
    i = pl.program_id(0)
    c = lax.rem(i, steps_per_seq)

    @pl.when(i == 0)
    def _():
        _project(x_ref[...], w, sets[0])

    @pl.when(c == 0)
    def _():
        xpad0[:, 0:CONV_TAIL, :] = jnp.zeros((CONV_SLABS, CONV_TAIL, LANES), jnp.float32)
        upad0[:, 0:POOL_TAIL, :] = jnp.zeros((POOL_SLABS, POOL_TAIL, LANES), jnp.float32)
        state_ref[...] = jnp.zeros(state_ref.shape, jnp.float32)

    for parity in range(2):
        @pl.when(lax.rem(i, 2) == parity)
        def _():
            cur, nxt = sets[parity], sets[1 - parity]
            _project(xn_ref[...], w, nxt)
            _mix(c, cur, nxt, w, ya_ref, yb_ref, scr)


def _const_spec(a):
    return pl.BlockSpec(a.shape, lambda i: (0,) * a.ndim, pipeline_mode=pl.Buffered(1))


def _mixer_call(x2d, weights, steps_per_seq):
    t = x2d.shape[0]
    n = t // STEP
    assert steps_per_seq % 2 == 0
    buf_set = [
        pltpu.VMEM((CONV_SLABS, CONV_TAIL + STEP, LANES), jnp.float32),
        pltpu.VMEM((POOL_SLABS, POOL_TAIL + STEP, LANES), jnp.float32),
        pltpu.VMEM((STEP, N_GATES), jnp.float32),
        pltpu.VMEM((STEP, LANES), jnp.float32),
    ]
    return pl.pallas_call(
        functools.partial(_mixer_kernel, steps_per_seq),
        grid=(n,),
        in_specs=[
            pl.BlockSpec((STEP, D_MODEL), lambda i: (0, 0)),
            pl.BlockSpec((STEP, D_MODEL), lambda i: (jnp.minimum(i + 1, n - 1), 0)),
        ] + [_const_spec(a) for a in weights],
        out_specs=[
            pl.BlockSpec((STEP, D_INNER), lambda i: (i, 0)),
            pl.BlockSpec((STEP, D_POOL), lambda i: (i, 0)),
        ],
        out_shape=[
            jax.ShapeDtypeStruct((t, D_INNER), jnp.bfloat16),
            jax.ShapeDtypeStruct((t, D_POOL), jnp.bfloat16),
        ],
        scratch_shapes=buf_set + buf_set + [
            pltpu.VMEM((2 * POOL_SLABS_PER_GROUP, 16 + STEP, LANES), jnp.float32),
            pltpu.VMEM((POOL_SLABS_PER_GROUP, 8 + STEP, LANES), jnp.float32),
            pltpu.VMEM((N_GROUPS, D_STATE, GROUP_WIDTH), jnp.float32),
        ],
        compiler_params=pltpu.CompilerParams(
            dimension_semantics=("arbitrary",),
            vmem_limit_bytes=VMEM_LIMIT_BYTES),
        name="mixer",
    )(x2d, x2d, *weights)


TAIL_WEIGHTS = ("norm_g", "w_zp", "w_ga", "w_gb", "w_a", "w_b", "w_out", "ple_g", "w_gate", "w_up",
                "final_g")


def _tail_kernel(x_ref, p_ref, ya_ref, yb_ref, *refs):
    nw = len(TAIL_WEIGHTS)
    w = dict(zip(TAIL_WEIGHTS, refs[:nw]))
    out_ref, h_ref, merged_ref, hg_ref, ybg_ref = refs[nw:]
    f32, bf16 = jnp.float32, jnp.bfloat16
    rows = x_ref.shape[0]
    tiles = [slice(n * TAIL_TN, (n + 1) * TAIL_TN) for n in range(D_MODEL // TAIL_TN)]

    h_ref[...] = (_rms_scale(x_ref[...]) * w["norm_g"][...]).astype(bf16)
    for ts in tiles:
        zp = jnp.dot(h_ref[...], w["w_zp"][:, ts], preferred_element_type=f32)
        ybg_ref[:, ts] = (yb_ref[:, ts].astype(f32) * _silu(zp)).astype(bf16)
    for ts in tiles:
        g_a = jnp.dot(h_ref[...], w["w_ga"][:, ts], preferred_element_type=f32)
        g_b = jnp.dot(h_ref[...], w["w_gb"][:, ts], preferred_element_type=f32)
        proj_a = jnp.dot(ya_ref[...], w["w_a"][:, ts], preferred_element_type=f32)
        proj_b = jnp.dot(ybg_ref[...], w["w_b"][:, ts], preferred_element_type=f32)
        merged_ref[:, ts] = (_sigmoid(g_a) * proj_a + _sigmoid(g_b) * proj_b).astype(bf16)
    ssq = jnp.zeros((rows, 1), f32)
    for ts in tiles:
        x1 = x_ref[:, ts] + jnp.dot(merged_ref[...], w["w_out"][:, ts], preferred_element_type=f32)
        out_ref[:, ts] = x1
        ssq = ssq + jnp.sum(x1 * x1, axis=-1, keepdims=True)

    inv = lax.rsqrt(ssq * (1.0 / D_MODEL) + EPS)
    for ts in tiles:
        hg_ref[:, ts] = (out_ref[:, ts] * inv * w["ple_g"][:, ts]).astype(bf16)
    p_bf = p_ref[...].astype(bf16)
    ssq = jnp.zeros((rows, 1), f32)
    for ts in tiles:
        gate = _sigmoid(jnp.dot(hg_ref[...], w["w_gate"][:, ts], preferred_element_type=f32))
        up = jnp.dot(p_bf, w["w_up"][:, ts], preferred_element_type=f32)
        x2 = out_ref[:, ts] + gate * up
        out_ref[:, ts] = x2
        ssq = ssq + jnp.sum(x2 * x2, axis=-1, keepdims=True)
    inv = lax.rsqrt(ssq * (1.0 / D_MODEL) + EPS)
    for ts in tiles:
        out_ref[:, ts] = out_ref[:, ts] * inv * w["final_g"][:, ts]


def _tail_call(x2d, p2d, ya, yb, weights):
    t = x2d.shape[0]
    tb = TAIL_ROWS if t % TAIL_ROWS == 0 else STEP
    row_spec = lambda width: pl.BlockSpec((tb, width), lambda i: (i, 0))
    return pl.pallas_call(
        _tail_kernel,
        grid=(t // tb,),
        in_specs=[row_spec(D_MODEL), row_spec(PLE_DIM), row_spec(D_INNER), row_spec(D_POOL)]
        + [_const_spec(a) for a in weights],
        out_specs=row_spec(D_MODEL),
        out_shape=jax.ShapeDtypeStruct((t, D_MODEL), jnp.float32),
        scratch_shapes=[
            pltpu.VMEM((tb, D_MODEL), jnp.bfloat16),
            pltpu.VMEM((tb, D_MODEL), jnp.bfloat16),
            pltpu.VMEM((tb, D_MODEL), jnp.bfloat16),
            pltpu.VMEM((tb, D_POOL), jnp.bfloat16),
        ],
        compiler_params=pltpu.CompilerParams(
            dimension_semantics=("arbitrary",),
            vmem_limit_bytes=VMEM_LIMIT_BYTES),
        name="tail",
    )(x2d, p2d, ya, yb, *weights)


def _row(v, width=None):
    v = v.reshape(1, -1).astype(jnp.float32)
    if width is not None and v.shape[1] < width:
        v = jnp.pad(v, ((0, 0), (0, width - v.shape[1])))
    return v


def kernel(x, p, norm_g, w_in, conv_w, conv_b, dt_bias, a_log, d_skip, gnorm_g, pool_mix_w,
           pool_mix_b, pool_scale, w_branch_a, w_branch_b, w_out, ple_norm_g, w_ple_gate,
           w_ple_up, final_g):
    depth = w_in.shape[0]
    bsz, seqlen, _ = x.shape
    assert seqlen % (2 * STEP) == 0
    assert depth == 1
    bf16 = jnp.bfloat16
    o_z, o_xbc, o_dt = 0, D_INNER, D_INNER + D_CONV_CH
    o_u = o_dt + N_HEADS
    o_zp, o_ga, o_gb = o_u + D_POOL, o_u + 2 * D_POOL, o_u + 3 * D_POOL
    t = bsz * seqlen
    i = 0
    wi = w_in.reshape(D_MODEL, -1)
    w_dt = jnp.pad(wi[:, o_dt:o_dt + N_HEADS], ((0, 0), (0, LANES - N_HEADS))).astype(bf16)
    mixer_weights = (
        _row(norm_g[i]), wi[:, o_xbc:o_xbc + D_CONV_CH].astype(bf16),
        wi[:, o_u:o_u + D_POOL].astype(bf16), wi[:, o_z:o_z + D_INNER].astype(bf16),
        w_dt,
        conv_w[i].astype(jnp.float32), _row(conv_b[i]), _row(dt_bias[i], LANES),
        _row(a_log[i], LANES), _row(jnp.repeat(d_skip[i], HEAD_DIM)), _row(gnorm_g[i]),
        pool_mix_w[i].astype(bf16), _row(pool_mix_b[i]), _row(pool_scale[i]),
    )
    tail_weights = (
        _row(norm_g[i]), wi[:, o_zp:o_zp + D_POOL].astype(bf16),
        wi[:, o_ga:o_ga + D_MODEL].astype(bf16),
        wi[:, o_gb:o_gb + D_MODEL].astype(bf16),
        w_branch_a[i].astype(bf16), w_branch_b[i].astype(bf16), w_out[i].astype(bf16),
        _row(ple_norm_g[i]), w_ple_gate[i].astype(bf16), w_ple_up[i].astype(bf16),
        _row(final_g),
    )
    x2d = x.reshape(t, D_MODEL)
    ya, yb = _mixer_call(x2d, mixer_weights, seqlen // STEP)
    out = _tail_call(x2d, p.reshape(depth * t, PLE_DIM), ya, yb, tail_weights)
    return out.reshape(bsz, seqlen, D_MODEL)
```
